```python
import jax
import jax.numpy as jnp
from jax import lax
import numpy as np

D_MODEL = 2048
BATCH = 4
SEQ = 4096
DEPTH = 2

MLA_HEADS = 8
MLA_NOPE_DIM = 128
MLA_ROPE_DIM = 64
MLA_V_DIM = 128
MLA_Q_RANK = 512
MLA_KV_RANK = 256
DSA_HEADS = 8
DSA_KV_HEADS = 2
DSA_HEAD_DIM = 128
DSA_ROPE_DIM = DSA_HEAD_DIM // 4
IDX_HEADS = 8
IDX_DIM = 64
IDX_ROPE_DIM = IDX_DIM // 4
TOPK_MAX = 256
CONV_WIDTH = 1024
CONV_K = 3
ROPE_THETA = 500000.0
D_FF_DENSE = 5632
N_EXPERTS = 8
TOP_K = 2
D_FF_EXPERT = 7168
Q_BLOCK = 128
MOE_BLOCK = 128
N_BRANCH = 3
ALPHA = (2 * DEPTH) ** 0.25
BETA = (8 * DEPTH) ** -0.25
LN_EPS = 1e-5
RMS_EPS = 1e-6
N_DENSE_LAYERS = (DEPTH + 1) // 2
N_MOE_LAYERS = DEPTH // 2
IN_SPLIT_SIZES = (MLA_Q_RANK, MLA_KV_RANK, MLA_ROPE_DIM,
                  DSA_HEADS * DSA_HEAD_DIM, DSA_KV_HEADS * DSA_HEAD_DIM, DSA_KV_HEADS * DSA_HEAD_DIM,
                  IDX_HEADS * IDX_DIM, IDX_DIM, IDX_HEADS,
                  CONV_WIDTH, CONV_WIDTH, CONV_WIDTH,
                  N_BRANCH * D_MODEL)
N_IN = sum(IN_SPLIT_SIZES)

kernel_name = "hybrid_mla_dsa_conv_moe_block"


def layer_norm(x, g, b):
    xf = x.astype(jnp.float32)
    mu = jnp.mean(xf, -1, keepdims=True)
    var = jnp.mean(jnp.square(xf - mu), -1, keepdims=True)
    y = (xf - mu) * lax.rsqrt(var + LN_EPS) * g.astype(jnp.float32) + b.astype(jnp.float32)
    return y.astype(x.dtype)


def rms_norm(x, g):
    xf = x.astype(jnp.float32)
    y = xf * lax.rsqrt(jnp.mean(jnp.square(xf), -1, keepdims=True) + RMS_EPS) * g.astype(jnp.float32)
    return y.astype(x.dtype)


def rope_tables(positions, rot_dim):
    inv = ROPE_THETA ** (-jnp.arange(0, rot_dim, 2, dtype=jnp.float32) / rot_dim)
    ang = positions.astype(jnp.float32)[..., None] * inv
    return jnp.cos(ang)[:, :, None, :], jnp.sin(ang)[:, :, None, :]


def apply_rope(x, cos, sin):
    half = cos.shape[-1]
    x1 = x[..., :half].astype(jnp.float32)
    x2 = x[..., half:2 * half].astype(jnp.float32)
    rot = jnp.concatenate([x1 * cos - x2 * sin, x2 * cos + x1 * sin], -1).astype(x.dtype)
    return jnp.concatenate([rot, x[..., 2 * half:]], -1)


def to_blocks(a):
    b, s = a.shape[:2]
    return jnp.moveaxis(a.reshape(b, s // Q_BLOCK, Q_BLOCK, *a.shape[2:]), 1, 0)


def from_blocks(a):
    nb, b, qb = a.shape[:3]
    return jnp.moveaxis(a, 0, 1).reshape(b, nb * qb, *a.shape[3:])


def mla_attention(q, k, v):
    s_len, dq = q.shape[1], q.shape[-1]
    scale = dq ** -0.5
    kpos = jnp.arange(s_len)

    def block(args):
        qb, start = args
        qpos = start + jnp.arange(Q_BLOCK)
        s = jnp.einsum('bqhd,bkhd->bhqk', qb, k).astype(jnp.float32) * scale
        s = jnp.where(kpos[None, :] <= qpos[:, None], s, -jnp.inf)
        p = jax.nn.softmax(s, axis=-1).astype(v.dtype)
        return jnp.einsum('bhqk,bkhd->bqhd', p, v)

    starts = jnp.arange(s_len // Q_BLOCK, dtype=jnp.int32) * Q_BLOCK
    return from_blocks(lax.map(block, (to_blocks(q), starts)))


def dsa_attention(q, k, v, qi, ki, wi, n_sel):
    bsz, s_len, n_h, hd = q.shape
    n_g = k.shape[2]
    n_r = n_h // n_g
    kpos = jnp.arange(s_len)
    gather = jax.vmap(lambda kk, ii: kk[ii])

    def block(args):
        qb, qib, wib, start = args
        qpos = start + jnp.arange(Q_BLOCK)
        logits = jnp.einsum('bqhd,bsd->bqhs', qib, ki).astype(jnp.float32) * (IDX_DIM ** -0.5)
        score = jnp.einsum('bqhs,bqh->bqs', jax.nn.relu(logits), wib.astype(jnp.float32))
        score = jnp.where((kpos[None, :] <= qpos[:, None])[None], score, -jnp.inf)
        _, sel = lax.top_k(score, n_sel)
        kb = gather(k, sel)
        vb = gather(v, sel)
        qg = qb.reshape(bsz, Q_BLOCK, n_g, n_r, hd)
        s = jnp.einsum('bqgrd,bqkgd->bgrqk', qg, kb).astype(jnp.float32) * (hd ** -0.5)
        valid = sel <= qpos[None, :, None]
        s = jnp.where(valid[:, None, None], s, -jnp.inf)
        p = jax.nn.softmax(s, axis=-1).astype(vb.dtype)
        o = jnp.einsum('bgrqk,bqkgd->bqgrd', p, vb)
        return o.reshape(bsz, Q_BLOCK, n_h, hd)

    starts = jnp.arange(s_len // Q_BLOCK, dtype=jnp.int32) * Q_BLOCK
    return from_blocks(lax.map(block, (to_blocks(q), to_blocks(qi), to_blocks(wi), starts)))


def short_conv(u, w, b):
    ch = u.shape[-1]
    y = lax.conv_general_dilated(u, w[:, None, :].astype(u.dtype), window_strides=(1,),
                                 padding=[(CONV_K - 1, 0)], dimension_numbers=('NWC', 'WIO', 'NWC'),
                                 feature_group_count=ch)
    return y + b.astype(u.dtype)


def token_mixer(h, rope_mla, rope_dsa, rope_idx, w_in, q_norm, kv_norm, w_uq, w_ukv, conv_w, conv_b,
                w_br_a, w_br_b, w_br_c, w_o):
    bsz, s_len, d = h.shape
    z = h @ w_in
    split_points = np.cumsum(IN_SPLIT_SIZES)[:-1].tolist()
    (cq, ckv, kr, q, k, v, qi, ki, wi, gb, gc, u, gates) = jnp.split(z, split_points, axis=-1)

    qa = (rms_norm(cq, q_norm) @ w_uq).reshape(bsz, s_len, MLA_HEADS, MLA_NOPE_DIM + MLA_ROPE_DIM)
    qa = jnp.concatenate([qa[..., :MLA_NOPE_DIM], apply_rope(qa[..., MLA_NOPE_DIM:], *rope_mla)], -1)
    kva = (rms_norm(ckv, kv_norm) @ w_ukv).reshape(bsz, s_len, MLA_HEADS, MLA_NOPE_DIM + MLA_V_DIM)
    k_rope = apply_rope(kr[:, :, None, :], *rope_mla)
    ka = jnp.concatenate([kva[..., :MLA_NOPE_DIM],
                          jnp.broadcast_to(k_rope, (bsz, s_len, MLA_HEADS, MLA_ROPE_DIM))], -1)
    ya = mla_attention(qa, ka, kva[..., MLA_NOPE_DIM:]).reshape(bsz, s_len, MLA_HEADS * MLA_V_DIM)

    qb = apply_rope(q.reshape(bsz, s_len, DSA_HEADS, DSA_HEAD_DIM), *rope_dsa)
    kb = apply_rope(k.reshape(bsz, s_len, DSA_KV_HEADS, DSA_HEAD_DIM), *rope_dsa)
    vb = v.reshape(bsz, s_len, DSA_KV_HEADS, DSA_HEAD_DIM)
    qi = apply_rope(qi.reshape(bsz, s_len, IDX_HEADS, IDX_DIM), *rope_idx)
    ki = apply_rope(ki[:, :, None, :], *rope_idx)[:, :, 0]
    n_sel = min(TOPK_MAX, s_len // 4)
    yb = dsa_attention(qb, kb, vb, qi, ki, wi * (IDX_HEADS ** -0.5), n_sel)
    yb = yb.reshape(bsz, s_len, DSA_HEADS * DSA_HEAD_DIM)

    yc = gb * short_conv(gc * u, conv_w, conv_b)

    g = jax.nn.sigmoid(gates.astype(jnp.float32)).astype(h.dtype).reshape(bsz, s_len, N_BRANCH, d)
    merged = g[:, :, 0] * (ya @ w_br_a) + g[:, :, 1] * (yb @ w_br_b) + g[:, :, 2] * (yc @ w_br_c)
    return merged @ w_o


def swiglu(h, w_gate, w_up, w_down):
    return (jax.nn.silu(h @ w_gate) * (h @ w_up)) @ w_down


def moe_swiglu(h, w_router, w_gate, w_up, w_down):
    bsz, s_len, d = h.shape
    xt = h.reshape(-1, d)
    n_tok = xt.shape[0]
    logits = (xt @ w_router).astype(jnp.float32)
    top_logit, top_e = lax.top_k(logits, TOP_K)
    top_g = jax.nn.softmax(top_logit, axis=-1)
    flat_e = top_e.reshape(-1).astype(jnp.int32)
    flat_tok = jnp.repeat(jnp.arange(n_tok, dtype=jnp.int32), TOP_K)
    flat_g = top_g.reshape(-1)
    order = jnp.argsort(flat_e)
    se = flat_e[order]
    counts = jnp.zeros(N_EXPERTS, jnp.int32).at[flat_e].add(1)
    padded = (counts + MOE_BLOCK - 1) // MOE_BLOCK * MOE_BLOCK
    start = jnp.cumsum(counts) - counts
    pstart = jnp.cumsum(padded) - padded
    pend = pstart + padded
    dest = pstart[se] + (jnp.arange(n_tok * TOP_K, dtype=jnp.int32) - start[se])
    n_rows = n_tok * TOP_K + N_EXPERTS * MOE_BLOCK
    n_blocks = n_rows // MOE_BLOCK
    row_tok = jnp.zeros(n_rows, jnp.int32).at[dest].set(flat_tok[order])
    row_gate = jnp.zeros(n_rows, jnp.float32).at[dest].set(flat_g[order])
    blk_start = jnp.arange(n_blocks, dtype=jnp.int32) * MOE_BLOCK
    blk_e = jnp.minimum(jnp.searchsorted(pend, blk_start, side='right'), N_EXPERTS - 1)

    def block(args):
        tok, e = args
        xb = xt[tok]
        return (jax.nn.silu(xb @ w_gate[e]) * (xb @ w_up[e])) @ w_down[e]

    yb = lax.map(block, (row_tok.reshape(n_blocks, MOE_BLOCK), blk_e)).reshape(n_rows, d)
    yb = yb * row_gate[:, None].astype(yb.dtype)
    y = jnp.zeros_like(xt).at[row_tok].add(yb)
    return y.reshape(bsz, s_len, d)


def setup_inputs(seed: int = 0) -> dict:
    key = jax.random.key(seed)
    ks = iter(jax.random.split(key, 40))
    f32 = jnp.float32

    def nrm(shape, scale):
        return jax.random.normal(next(ks), shape, f32) * scale

    def gain(shape):
        return 1.0 + nrm(shape, 0.02)

    d = D_MODEL
    x = nrm((BATCH, SEQ, d), 1.0)
    c = nrm((BATCH, d), 1.0)
    offs = jax.random.randint(next(ks), (BATCH, 1), 0, 1024, dtype=jnp.int32)
    positions = (offs + jnp.arange(SEQ, dtype=jnp.int32)[None, :]).astype(jnp.int32)
    return {
        'x': x,
        'c': c,
        'positions': positions,
        'ada_w': nrm((DEPTH, d, 6 * d), 0.5 * d ** -0.5),
        'ada_b': nrm((DEPTH, 6 * d), 0.02),
        'ln1_g': gain((DEPTH, d)),
        'ln1_b': nrm((DEPTH, d), 0.02),
        'ln2_g': gain((DEPTH, d)),
        'ln2_b': nrm((DEPTH, d), 0.02),
        'w_in': nrm((DEPTH, d, N_IN), d ** -0.5),
        'mla_q_norm': gain((DEPTH, MLA_Q_RANK)),
        'mla_kv_norm': gain((DEPTH, MLA_KV_RANK)),
        'w_uq': nrm((DEPTH, MLA_Q_RANK, MLA_HEADS * (MLA_NOPE_DIM + MLA_ROPE_DIM)), MLA_Q_RANK ** -0.5),
        'w_ukv': nrm((DEPTH, MLA_KV_RANK, MLA_HEADS * (MLA_NOPE_DIM + MLA_V_DIM)), MLA_KV_RANK ** -0.5),
        'conv_w': nrm((DEPTH, CONV_K, CONV_WIDTH), CONV_K ** -0.5),
        'conv_b': nrm((DEPTH, CONV_WIDTH), 0.02),
        'w_branch_a': nrm((DEPTH, MLA_HEADS * MLA_V_DIM, d), (MLA_HEADS * MLA_V_DIM) ** -0.5),
        'w_branch_b': nrm((DEPTH, DSA_HEADS * DSA_HEAD_DIM, d), (DSA_HEADS * DSA_HEAD_DIM) ** -0.5),
        'w_branch_c': nrm((DEPTH, CONV_WIDTH, d), CONV_WIDTH ** -0.5),
        'w_o': nrm((DEPTH, d, d), BETA * d ** -0.5),
        'ffn_w_gate': nrm((N_DENSE_LAYERS, d, D_FF_DENSE), d ** -0.5),
        'ffn_w_up': nrm((N_DENSE_LAYERS, d, D_FF_DENSE), d ** -0.5),
        'ffn_w_down': nrm((N_DENSE_LAYERS, D_FF_DENSE, d), BETA * D_FF_DENSE ** -0.5),
        'router_w': nrm((N_MOE_LAYERS, d, N_EXPERTS), d ** -0.5),
        'moe_w_gate': nrm((N_MOE_LAYERS, N_EXPERTS, d, D_FF_EXPERT), d ** -0.5),
        'moe_w_up': nrm((N_MOE_LAYERS, N_EXPERTS, d, D_FF_EXPERT), d ** -0.5),
        'moe_w_down': nrm((N_MOE_LAYERS, N_EXPERTS, D_FF_EXPERT, d), BETA * D_FF_EXPERT ** -0.5),
    }


def reference(x, c, positions, ada_w, ada_b, ln1_g, ln1_b, ln2_g, ln2_b, w_in, mla_q_norm, mla_kv_norm,
              w_uq, w_ukv, conv_w, conv_b, w_branch_a, w_branch_b, w_branch_c, w_o,
              ffn_w_gate, ffn_w_up, ffn_w_down, router_w, moe_w_gate, moe_w_up, moe_w_down):
    rope_mla = rope_tables(positions, MLA_ROPE_DIM)
    rope_dsa = rope_tables(positions, DSA_ROPE_DIM)
    rope_idx = rope_tables(positions, IDX_ROPE_DIM)
    c_act = jax.nn.silu(c)
    for i in range(DEPTH):
        mod = (c_act @ ada_w[i] + ada_b[i])[:, None, :]
        sh_m, sc_m, g_m, sh_f, sc_f, g_f = jnp.split(mod, 6, axis=-1)
        h = x * (1.0 + sc_m) + sh_m
        mix = token_mixer(h, rope_mla, rope_dsa, rope_idx, w_in[i], mla_q_norm[i], mla_kv_norm[i],
                          w_uq[i], w_ukv[i], conv_w[i], conv_b[i],
                          w_branch_a[i], w_branch_b[i], w_branch_c[i], w_o[i])
        x = layer_norm(ALPHA * x + g_m * mix, ln1_g[i], ln1_b[i])
        h = x * (1.0 + sc_f) + sh_f
        if i % 2 == 0:
            j = i // 2
            f = swiglu(h, ffn_w_gate[j], ffn_w_up[j], ffn_w_down[j])
        else:
            j = i // 2
            f = moe_swiglu(h, router_w[j], moe_w_gate[j], moe_w_up[j], moe_w_down[j])
        x = layer_norm(ALPHA * x + g_f * f, ln2_g[i], ln2_b[i])
    return x
```

```python
import functools

import jax
import jax.numpy as jnp
from jax import lax
from jax.experimental import pallas as pl
from jax.experimental.pallas import tpu as pltpu

F32 = jnp.float32
CDT = jnp.bfloat16

D = 2048
DEPTH = 2
H_MLA, NOPE, ROPE_MLA, V_MLA = 8, 128, 64, 128
Q_RANK, KV_RANK = 512, 256
H_DSA, G_DSA, HD_DSA = 8, 2, 128
ROT_DSA = HD_DSA // 4
H_IDX, D_IDX = 8, 64
ROT_IDX = D_IDX // 4
TOPK_MAX = 256
CONV_W, CONV_K = 1024, 3
THETA = 500000.0
N_EXPERTS, TOP_K = 8, 2
ALPHA = (2 * DEPTH) ** 0.25
LN_EPS = 1e-5
RMS_EPS = 1e-6
SPLITS = (Q_RANK, KV_RANK, ROPE_MLA, H_DSA * HD_DSA, G_DSA * HD_DSA, G_DSA * HD_DSA,
          H_IDX * D_IDX, D_IDX, H_IDX, CONV_W, CONV_W, CONV_W, 3 * D)

LANES = 128
NEG = -1e30
VMEM_LIMIT = 56 * 1024 * 1024

MOE_BLK = 512


def _params(sem, vmem=VMEM_LIMIT):
    return pltpu.CompilerParams(dimension_semantics=sem, vmem_limit_bytes=vmem)


def _sigmoid(x):
    return 1.0 / (1.0 + jnp.exp(-x))


def _dot(a, b):
    return jnp.dot(a, b, preferred_element_type=F32)


def _dot_t(a, b):
    return lax.dot_general(a, b, (((1,), (1,)), ((), ())), preferred_element_type=F32)


def _layer_norm(y, g, b):
    mu = jnp.mean(y, -1, keepdims=True)
    d = y - mu
    var = jnp.mean(d * d, -1, keepdims=True)
    return d * lax.rsqrt(var + LN_EPS) * g + b


def _rope(y, c, s, half):
    lane = lax.broadcasted_iota(jnp.int32, y.shape, 1)
    swapped = jnp.where(lane < half, pltpu.roll(y, LANES - half, 1), pltpu.roll(y, half, 1))
    return y * c + swapped * s


def _ada_body(c_ref, w_ref, b_ref, o_ref):
    c = c_ref[...]
    ca = (c * _sigmoid(c)).astype(CDT)
    o_ref[...] = _dot(ca, w_ref[...].astype(CDT)) + b_ref[...]


def _ada(c, ada_w, ada_b):
    depth, d, n = ada_w.shape
    bsz = c.shape[0]
    rows = 8
    cp = jnp.zeros((rows, d), F32).at[:bsz].set(c)
    tn = 1024
    out = pl.pallas_call(
        _ada_body,
        grid=(depth, n // tn),
        in_specs=[pl.BlockSpec((rows, d), lambda l, j: (0, 0)),
                  pl.BlockSpec((None, d, tn), lambda l, j: (l, 0, j)),
                  pl.BlockSpec((None, 1, tn), lambda l, j: (l, 0, j))],
        out_specs=pl.BlockSpec((None, rows, tn), lambda l, j: (l, 0, j)),
        out_shape=jax.ShapeDtypeStruct((depth, rows, n), F32),
        compiler_params=_params(("arbitrary", "arbitrary")),
        name="ada_mod",
    )(cp, ada_w, ada_b.reshape(depth, 1, n))
    return out[:, :bsz]


def _mod_body(x_ref, sc_ref, sh_ref, o_ref):
    o_ref[...] = (x_ref[...] * (1.0 + sc_ref[...]) + sh_ref[...]).astype(o_ref.dtype)


def _modulate(x, sc, sh, out_dtype):
    bsz, s_len, d = x.shape
    tm = 512
    vec = pl.BlockSpec((None, 1, d), lambda b, i: (b, 0, 0))
    return pl.pallas_call(
        _mod_body,
        grid=(bsz, s_len // tm),
        in_specs=[pl.BlockSpec((None, tm, d), lambda b, i: (b, i, 0)), vec, vec],
        out_specs=pl.BlockSpec((None, tm, d), lambda b, i: (b, i, 0)),
        out_shape=jax.ShapeDtypeStruct(x.shape, out_dtype),
        compiler_params=_params(("arbitrary", "arbitrary")),
        name="modulate",
    )(x, sc, sh)


def _mla_prep_body(h_ref, w1_ref, qg_ref, kg_ref, wuq_ref, wukv_ref, c_ref, s_ref, q_ref, k_ref, v_ref):
    z = _dot(h_ref[...], w1_ref[...])
    cq = z[:, :Q_RANK]
    ckv = z[:, Q_RANK:Q_RANK + KV_RANK]
    kr = z[:, Q_RANK + KV_RANK:]
    nq = (cq * lax.rsqrt(jnp.mean(cq * cq, -1, keepdims=True) + RMS_EPS) * qg_ref[...]).astype(CDT)
    nkv = (ckv * lax.rsqrt(jnp.mean(ckv * ckv, -1, keepdims=True) + RMS_EPS) * kg_ref[...]).astype(CDT)
    qa = _dot(nq, wuq_ref[...])
    kva = _dot(nkv, wukv_ref[...])
    c = c_ref[...]
    s = s_ref[...]
    scale = (NOPE + ROPE_MLA) ** -0.5
    krot = _rope(kr, c, s, ROPE_MLA // 2).astype(CDT)
    for h in range(H_MLA):
        lo = h * 2 * LANES
        q_ref[h, :, 0:LANES] = (qa[:, lo:lo + LANES] * scale).astype(CDT)
        q_ref[h, :, LANES:2 * LANES] = (_rope(qa[:, lo + LANES:lo + 2 * LANES], c, s, ROPE_MLA // 2)
                                        * scale).astype(CDT)
        k_ref[h, :, 0:LANES] = kva[:, lo:lo + LANES].astype(CDT)
        k_ref[h, :, LANES:2 * LANES] = krot
        v_ref[h] = kva[:, lo + LANES:lo + 2 * LANES].astype(CDT)


def _mla_prep(h, w1, qg, kg, wuq, wukv, ctab, stab):
    bsz, s_len, d = h.shape
    tm = 512
    n1 = w1.shape[1]
    full = lambda a: pl.BlockSpec(a.shape, lambda b, i: (0,) * a.ndim)
    row = lambda w: pl.BlockSpec((None, tm, w), lambda b, i: (b, i, 0))
    head = lambda w: pl.BlockSpec((None, H_MLA, tm, w), lambda b, i: (b, 0, i, 0))
    return pl.pallas_call(
        _mla_prep_body,
        grid=(bsz, s_len // tm),
        in_specs=[row(d), full(w1), full(qg), full(kg), full(wuq), full(wukv), row(LANES), row(LANES)],
        out_specs=[head(2 * LANES), head(2 * LANES), head(LANES)],
        out_shape=[jax.ShapeDtypeStruct((bsz, H_MLA, s_len, 2 * LANES), CDT),
                   jax.ShapeDtypeStruct((bsz, H_MLA, s_len, 2 * LANES), CDT),
                   jax.ShapeDtypeStruct((bsz, H_MLA, s_len, LANES), CDT)],
        compiler_params=_params(("arbitrary", "arbitrary")),
        name="mla_prep",
    )(h, w1, qg, kg, wuq, wukv, ctab, stab)


def _flash_body(q_ref, k_ref, v_ref, o_ref, *, blk):
    qi = pl.program_id(2)
    q = q_ref[...]

    def step(j, carry, masked):
        m, l, acc = carry
        off = pl.multiple_of(j * blk, blk)
        k = k_ref[pl.ds(off, blk), :]
        v = v_ref[pl.ds(off, blk), :]
        s = _dot_t(q, k)
        if masked:
            row = lax.broadcasted_iota(jnp.int32, s.shape, 0)
            col = lax.broadcasted_iota(jnp.int32, s.shape, 1)
            s = jnp.where(col <= row, s, NEG)
        m_new = jnp.maximum(m, jnp.max(s, -1, keepdims=True))
        alpha = jnp.exp(m - m_new)
        p = jnp.exp(s - m_new)
        l = alpha * l + jnp.sum(p, -1, keepdims=True)
        acc = alpha * acc + _dot(p.astype(CDT), v)
        return m_new, l, acc

    init = (jnp.full((blk, 1), NEG, F32), jnp.zeros((blk, 1), F32), jnp.zeros((blk, v_ref.shape[-1]), F32))
    carry = lax.fori_loop(0, qi, lambda j, c: step(j, c, False), init)
    _, l, acc = step(qi, carry, True)
    o_ref[...] = (acc / l).astype(o_ref.dtype)


def _mla_attention(q, k, v):
    bsz, nh, s_len, dq = q.shape
    dv = v.shape[-1]
    blk = min(512, s_len)
    return pl.pallas_call(
        functools.partial(_flash_body, blk=blk),
        grid=(bsz, nh, s_len // blk),
        in_specs=[pl.BlockSpec((None, None, blk, dq), lambda b, h, i: (b, h, i, 0)),
                  pl.BlockSpec((None, None, s_len, dq), lambda b, h, i: (b, h, 0, 0)),
                  pl.BlockSpec((None, None, s_len, dv), lambda b, h, i: (b, h, 0, 0))],
        out_specs=pl.BlockSpec((None, blk, dv), lambda b, h, i: (b, i, h)),
        out_shape=jax.ShapeDtypeStruct((bsz, s_len, nh * dv), CDT),
        compiler_params=_params(("arbitrary", "arbitrary", "arbitrary")),
        name="mla_attention",
    )(q, k, v)


def _dsa_prep_body(h_ref, w2_ref, cd_ref, sd_ref, ci_ref, si_ref, q_ref, k_ref, v_ref, qi_ref, ki_ref, wi_ref):
    z = _dot(h_ref[...], w2_ref[...])
    cd, sd, ci, si = cd_ref[...], sd_ref[...], ci_ref[...], si_ref[...]
    col = lambda j: z[:, j * LANES:(j + 1) * LANES]
    for h in range(H_DSA):
        q_ref[:, h * LANES:(h + 1) * LANES] = (_rope(col(h), cd, sd, ROT_DSA // 2) * HD_DSA ** -0.5).astype(CDT)
    for g in range(G_DSA):
        k_ref[:, g * LANES:(g + 1) * LANES] = _rope(col(H_DSA + g), cd, sd, ROT_DSA // 2).astype(CDT)
    base = H_DSA + G_DSA
    v_ref[...] = z[:, base * LANES:(base + G_DSA) * LANES].astype(CDT)
    base += G_DSA
    for h in range(H_IDX):
        qi_ref[h] = (_rope(col(base + h), ci, si, ROT_IDX // 2) * D_IDX ** -0.5).astype(CDT)
    base += H_IDX
    ki_ref[...] = _rope(col(base), ci, si, ROT_IDX // 2).astype(CDT)
    wi_ref[...] = col(base + 1) * H_IDX ** -0.5


def _dsa_prep(h, w2, cd, sd, ci, si):
    bsz, s_len, d = h.shape
    tm = 512
    row = lambda w: pl.BlockSpec((None, tm, w), lambda b, i: (b, i, 0))
    return pl.pallas_call(
        _dsa_prep_body,
        grid=(bsz, s_len // tm),
        in_specs=[row(d), pl.BlockSpec(w2.shape, lambda b, i: (0, 0)), row(LANES), row(LANES), row(LANES),
                  row(LANES)],
        out_specs=[row(H_DSA * LANES), row(G_DSA * LANES), row(G_DSA * LANES),
                   pl.BlockSpec((H_IDX, None, tm, LANES), lambda b, i: (0, b, i, 0)), row(LANES), row(LANES)],
        out_shape=[jax.ShapeDtypeStruct((bsz, s_len, H_DSA * LANES), CDT),
                   jax.ShapeDtypeStruct((bsz, s_len, G_DSA * LANES), CDT),
                   jax.ShapeDtypeStruct((bsz, s_len, G_DSA * LANES), CDT),
                   jax.ShapeDtypeStruct((H_IDX, bsz, s_len, LANES), CDT),
                   jax.ShapeDtypeStruct((bsz, s_len, LANES), CDT),
                   jax.ShapeDtypeStruct((bsz, s_len, LANES), F32)],
        compiler_params=_params(("arbitrary", "arbitrary")),
        name="dsa_prep",
    )(h, w2, cd, sd, ci, si)


def _lane_fold(x):
    out = x[:, 0:LANES]
    for j in range(1, x.shape[1] // LANES):
        out = out + x[:, j * LANES:(j + 1) * LANES]
    return out


def _dsa_body(qi_ref, ki_ref, wi_ref, q_ref, k_ref, v_ref, tri_ref, o_ref,
              key_ref, sel_ref, m_ref, l_ref, acc_ref, *, tq, tkc, n_sel):
    i = pl.program_id(1)
    q0 = i * tq
    nkc = (q0 + tq + tkc - 1) // tkc
    int_min = jnp.iinfo(jnp.int32).min
    key_neg_inf = jnp.int32(-8388608) ^ jnp.int32(0x7FFFFFFF)
    rowpos = q0 + lax.broadcasted_iota(jnp.int32, (tq, tkc), 0)

    def score_chunk(c, _):
        off = pl.multiple_of(c * tkc, tkc)
        kc = ki_ref[pl.ds(off, tkc), :]
        w = wi_ref[...]
        sc = jnp.zeros((tq, tkc), F32)
        for h in range(H_IDX):
            sc = sc + w[:, h:h + 1] * jnp.maximum(_dot_t(qi_ref[h], kc), 0.0)
        colpos = off + lax.broadcasted_iota(jnp.int32, (tq, tkc), 1)
        sc = jnp.where(colpos <= rowpos, sc, -jnp.inf)
        bits = pltpu.bitcast(sc, jnp.int32)
        key_ref[c] = jnp.where(bits >= 0, bits, bits ^ jnp.int32(0x7FFFFFFF))
        return 0

    lax.fori_loop(0, nkc, score_chunk, 0)

    def count(pred):
        def body(c, cnt):
            return cnt + _lane_fold(jnp.where(pred(key_ref[c]), 1.0, 0.0))
        part = lax.fori_loop(0, nkc, body, jnp.zeros((tq, LANES), F32))
        return jnp.sum(part, -1, keepdims=True)

    def bit_step(bi, thr):
        cand = thr + lax.shift_left(jnp.int32(1), 31 - bi)
        cnt = count(lambda kk: kk >= cand)
        return jnp.where(cnt >= n_sel, cand, thr)

    thr = lax.fori_loop(0, 32, bit_step, jnp.full((tq, 1), int_min, jnp.int32))

    cnt_gt = count(lambda kk: kk > thr)
    cnt_eq = count(lambda kk: kk == thr)
    need = n_sel - cnt_gt
    excess = jnp.where(thr > key_neg_inf, cnt_eq - need, 0.0)

    def mask_plain():
        def body(c, _):
            kk = key_ref[c]
            sel_ref[c] = jnp.where(kk >= thr, jnp.where(kk > key_neg_inf, 0.0, NEG), NEG)
            return 0
        lax.fori_loop(0, nkc, body, 0)

    def mask_ranked():
        def body(c, seen):
            kk = key_ref[c]
            tie = kk == thr
            pre = _dot(jnp.where(tie, 1.0, 0.0).astype(CDT), tri_ref[...])
            rank = seen + pre
            keep_tie = jnp.where(tie, jnp.where(rank <= need, 0.0, NEG), NEG)
            keep = jnp.where(kk > thr, 0.0, keep_tie)
            sel_ref[c] = jnp.where(kk > key_neg_inf, keep, NEG)
            return seen + pre[:, tkc - 1:tkc]
        lax.fori_loop(0, nkc, body, jnp.zeros((tq, 1), F32))

    lax.cond(jnp.max(excess) > 0.0, mask_ranked, mask_plain)

    n_rep = H_DSA // G_DSA
    m_ref[...] = jnp.full(m_ref.shape, NEG, F32)
    l_ref[...] = jnp.zeros(l_ref.shape, F32)
    acc_ref[...] = jnp.zeros(acc_ref.shape, F32)

    def att_chunk(c, _):
        off = pl.multiple_of(c * tkc, tkc)
        msk = sel_ref[c]
        msk = jnp.concatenate([msk] * n_rep, axis=0)
        for g in range(G_DSA):
            qg = jnp.concatenate([q_ref[:, (g * n_rep + r) * LANES:(g * n_rep + r + 1) * LANES]
                                  for r in range(n_rep)], axis=0)
            kg = k_ref[pl.ds(off, tkc), g * LANES:(g + 1) * LANES]
            vg = v_ref[pl.ds(off, tkc), g * LANES:(g + 1) * LANES]
            s = _dot_t(qg, kg) + msk
            m_old = m_ref[g]
            m_new = jnp.maximum(m_old, jnp.max(s, -1, keepdims=True))
            alpha = jnp.exp(m_old - m_new)
            p = jnp.exp(s - m_new)
            l_ref[g] = alpha * l_ref[g] + jnp.sum(p, -1, keepdims=True)
            acc_ref[g] = alpha * acc_ref[g] + _dot(p.astype(CDT), vg)
            m_ref[g] = m_new
        return 0

    lax.fori_loop(0, nkc, att_chunk, 0)
    for g in range(G_DSA):
        for r in range(n_rep):
            hh = g * n_rep + r
            o_ref[:, hh * LANES:(hh + 1) * LANES] = (acc_ref[g, r * tq:(r + 1) * tq, :]
                                                     / l_ref[g, r * tq:(r + 1) * tq, :]).astype(o_ref.dtype)


def _dsa_attention(qi, ki, wi, q, k, v):
    bsz, s_len, _ = q.shape
    tq = 128
    tkc = min(512, s_len)
    n_sel = min(TOPK_MAX, s_len // 4)
    n_rep = H_DSA // G_DSA
    tri = (lax.broadcasted_iota(jnp.int32, (tkc, tkc), 0)
           <= lax.broadcasted_iota(jnp.int32, (tkc, tkc), 1)).astype(CDT)
    row = lambda w: pl.BlockSpec((None, tq, w), lambda b, i: (b, i, 0))
    seq = lambda w: pl.BlockSpec((None, s_len, w), lambda b, i: (b, 0, 0))
    return pl.pallas_call(
        functools.partial(_dsa_body, tq=tq, tkc=tkc, n_sel=n_sel),
        grid=(bsz, s_len // tq),
        in_specs=[pl.BlockSpec((H_IDX, None, tq, LANES), lambda b, i: (0, b, i, 0)), seq(LANES), row(LANES),
                  row(H_DSA * LANES), seq(G_DSA * LANES), seq(G_DSA * LANES),
                  pl.BlockSpec((tkc, tkc), lambda b, i: (0, 0))],
        out_specs=row(H_DSA * LANES),
        out_shape=jax.ShapeDtypeStruct((bsz, s_len, H_DSA * LANES), CDT),
        scratch_shapes=[pltpu.VMEM((s_len // tkc, tq, tkc), jnp.int32),
                        pltpu.VMEM((s_len // tkc, tq, tkc), F32),
                        pltpu.VMEM((G_DSA, n_rep * tq, 1), F32),
                        pltpu.VMEM((G_DSA, n_rep * tq, 1), F32),
                        pltpu.VMEM((G_DSA, n_rep * tq, LANES), F32)],
        compiler_params=_params(("arbitrary", "arbitrary")),
        name="dsa_attention",
    )(qi, ki, wi, q, k, v, tri)


def _conv_body(h_ref, w_ref, cw_ref, cb_ref, o_ref, tail_ref, *, tm, tc):
    @pl.when(pl.program_id(2) == 0)
    def _():
        tail_ref[...] = jnp.zeros(tail_ref.shape, F32)

    z = _dot(h_ref[...], w_ref[...])
    gb = z[:, :tc]
    u = z[:, tc:2 * tc] * z[:, 2 * tc:]
    prev = tail_ref[...]
    row = lax.broadcasted_iota(jnp.int32, (tm, tc), 0)
    u1 = jnp.where(row == 0, prev[7:8, :], pltpu.roll(u, 1, 0))
    u2 = jnp.where(row == 0, prev[6:7, :], jnp.where(row == 1, prev[7:8, :], pltpu.roll(u, 2, 0)))
    cw = cw_ref[...]
    y = cw[0:1, :] * u2 + cw[1:2, :] * u1 + cw[2:3, :] * u + cb_ref[...]
    o_ref[...] = (gb * y).astype(o_ref.dtype)
    tail_ref[...] = u[tm - 8:tm, :]


def _conv_branch(h, w3, conv_w, conv_b, tc):
    bsz, s_len, d = h.shape
    tm = 512
    n_cb = CONV_W // tc
    return pl.pallas_call(
        functools.partial(_conv_body, tm=tm, tc=tc),
        grid=(n_cb, bsz, s_len // tm),
        in_specs=[pl.BlockSpec((None, tm, d), lambda cb, b, i: (b, i, 0)),
                  pl.BlockSpec((d, 3 * tc), lambda cb, b, i: (0, cb)),
                  pl.BlockSpec((CONV_K, tc), lambda cb, b, i: (0, cb)),
                  pl.BlockSpec((1, tc), lambda cb, b, i: (0, cb))],
        out_specs=pl.BlockSpec((None, tm, tc), lambda cb, b, i: (b, i, cb)),
        out_shape=jax.ShapeDtypeStruct((bsz, s_len, CONV_W), CDT),
        scratch_shapes=[pltpu.VMEM((8, tc), F32)],
        compiler_params=_params(("arbitrary", "arbitrary", "arbitrary")),
        name="conv_branch",
    )(h, w3, conv_w, conv_b.reshape(1, CONV_W))


def _merge_body(h_ref, ya_ref, yb_ref, yc_ref, wg_ref, wbr_ref, o_ref):
    h = h_ref[...]
    acc = None
    for j, y_ref in enumerate((ya_ref, yb_ref, yc_ref)):
        term = _sigmoid(_dot(h, wg_ref[j])) * _dot(y_ref[...], wbr_ref[j])
        acc = term if acc is None else acc + term
    o_ref[...] = acc.astype(o_ref.dtype)


def _merge(h, ya, yb, yc, wg, wbr):
    t, d = h.shape
    tm, tn = 512, 512
    yw = ya.shape[1]
    row = lambda w: pl.BlockSpec((tm, w), lambda j, i: (i, 0))
    return pl.pallas_call(
        _merge_body,
        grid=(d // tn, t // tm),
        in_specs=[row(d), row(yw), row(yw), row(yw),
                  pl.BlockSpec((3, d, tn), lambda j, i: (0, 0, j)),
                  pl.BlockSpec((3, yw, tn), lambda j, i: (0, 0, j))],
        out_specs=pl.BlockSpec((tm, tn), lambda j, i: (i, j)),
        out_shape=jax.ShapeDtypeStruct((t, d), CDT),
        compiler_params=_params(("arbitrary", "arbitrary")),
        name="gated_merge",
    )(h, ya, yb, yc, wg, wbr)


def _out_ln_body(m_ref, wo_ref, x_ref, gm_ref, lg_ref, lb_ref, sc_ref, sh_ref, x1_ref, h_ref):
    y = ALPHA * x_ref[...] + gm_ref[...] * _dot(m_ref[...], wo_ref[...])
    x1 = _layer_norm(y, lg_ref[...], lb_ref[...])
    x1_ref[...] = x1
    h_ref[...] = (x1 * (1.0 + sc_ref[...]) + sh_ref[...]).astype(h_ref.dtype)


def _out_ln(merged, wo, x, gm, lg, lb, sc, sh, h_dtype):
    bsz, s_len, d = x.shape
    tm = 256
    row = pl.BlockSpec((None, tm, d), lambda b, i: (b, i, 0))
    per_b = pl.BlockSpec((None, 1, d), lambda b, i: (b, 0, 0))
    vec = pl.BlockSpec((1, d), lambda b, i: (0, 0))
    return pl.pallas_call(
        _out_ln_body,
        grid=(bsz, s_len // tm),
        in_specs=[row, pl.BlockSpec((d, d), lambda b, i: (0, 0)), row, per_b, vec, vec, per_b, per_b],
        out_specs=[row, row],
        out_shape=[jax.ShapeDtypeStruct(x.shape, F32), jax.ShapeDtypeStruct(x.shape, h_dtype)],
        compiler_params=_params(("arbitrary", "arbitrary")),
        name="out_proj_ln",
    )(merged, wo, x, gm, lg, lb, sc, sh)


def _ffn_body(h_ref, wg_ref, wu_ref, wd_ref, x_ref, gf_ref, lg_ref, lb_ref, sc_ref, sh_ref, x2_ref, h2_ref,
              acc_ref):
    k = pl.program_id(2)

    @pl.when(k == 0)
    def _():
        acc_ref[...] = jnp.zeros(acc_ref.shape, F32)

    h = h_ref[...]
    a = _dot(h, wg_ref[...])
    act = (a * _sigmoid(a) * _dot(h, wu_ref[...])).astype(CDT)
    acc_ref[...] += _dot(act, wd_ref[...])

    @pl.when(k == pl.num_programs(2) - 1)
    def _():
        y = ALPHA * x_ref[...] + gf_ref[...] * acc_ref[...]
        x2 = _layer_norm(y, lg_ref[...], lb_ref[...])
        x2_ref[...] = x2
        h2_ref[...] = (x2 * (1.0 + sc_ref[...]) + sh_ref[...]).astype(h2_ref.dtype)


def _ffn_ln(h, wg, wu, wd, x, gf, lg, lb, sc, sh):
    bsz, s_len, d = x.shape
    f = wg.shape[1]
    tm, tf = 512, 512
    row = pl.BlockSpec((None, tm, d), lambda b, i, k: (b, i, 0))
    per_b = pl.BlockSpec((None, 1, d), lambda b, i, k: (b, 0, 0))
    vec = pl.BlockSpec((1, d), lambda b, i, k: (0, 0))
    return pl.pallas_call(
        _ffn_body,
        grid=(bsz, s_len // tm, f // tf),
        in_specs=[row, pl.BlockSpec((d, tf), lambda b, i, k: (0, k)), pl.BlockSpec((d, tf), lambda b, i, k: (0, k)),
                  pl.BlockSpec((tf, d), lambda b, i, k: (k, 0)), row, per_b, vec, vec, per_b, per_b],
        out_specs=[row, row],
        out_shape=[jax.ShapeDtypeStruct(x.shape, F32), jax.ShapeDtypeStruct(x.shape, CDT)],
        scratch_shapes=[pltpu.VMEM((tm, d), F32)],
        compiler_params=_params(("arbitrary", "arbitrary", "arbitrary")),
        name="ffn_ln",
    )(h, wg, wu, wd, x, gf, lg, lb, sc, sh)


def _router_body(h_ref, wr_ref, e_ref, g_ref):
    lg = jnp.dot(h_ref[...], wr_ref[...], preferred_element_type=F32, precision=lax.Precision.HIGHEST)
    lane = lax.broadcasted_iota(jnp.int32, lg.shape, 1)
    lanef = lane.astype(F32)
    lg = jnp.where(lane < N_EXPERTS, lg, -jnp.inf)
    m1 = jnp.max(lg, -1, keepdims=True)
    i1 = jnp.min(jnp.where(lg == m1, lanef, float(LANES)), -1, keepdims=True)
    lg2 = jnp.where(lanef == i1, -jnp.inf, lg)
    m2 = jnp.max(lg2, -1, keepdims=True)
    i2 = jnp.min(jnp.where(lg2 == m2, lanef, float(LANES)), -1, keepdims=True)
    e = jnp.exp(m2 - m1)
    g1 = 1.0 / (1.0 + e)
    g2 = e / (1.0 + e)
    e_ref[...] = jnp.where(lane == 0, i1, jnp.where(lane == 1, i2, 0.0)).astype(jnp.int32)
    g_ref[...] = jnp.where(lane == 0, g1, jnp.where(lane == 1, g2, 0.0))


def _router(h, wr):
    t, d = h.shape
    tm = 512
    return pl.pallas_call(
        _router_body,
        grid=(t // tm,),
        in_specs=[pl.BlockSpec((tm, d), lambda i: (i, 0)), pl.BlockSpec((d, LANES), lambda i: (0, 0))],
        out_specs=[pl.BlockSpec((tm, LANES), lambda i: (i, 0)), pl.BlockSpec((tm, LANES), lambda i: (i, 0))],
        out_shape=[jax.ShapeDtypeStruct((t, LANES), jnp.int32), jax.ShapeDtypeStruct((t, LANES), F32)],
        compiler_params=_params(("arbitrary",)),
        name="moe_router",
    )(h, wr)


def _moe_body(tok_ref, be_ref, bv_ref, h_hbm, wg_ref, wu_ref, wd_ref, y_ref, xbuf, xs, acc_ref, sem, *, blk):
    i = pl.program_id(0)
    k = pl.program_id(1)
    nb = pl.num_programs(0)
    nk = pl.num_programs(1)

    def row_copy(tok, slot, r):
        return pltpu.make_async_copy(h_hbm.at[pl.ds(tok, 1), :], xbuf.at[slot, pl.ds(r, 1), :], sem.at[slot])

    def issue(bi, slot):
        def body(r, _):
            row_copy(tok_ref[bi * blk + r], slot, r).start()
            return 0
        lax.fori_loop(0, blk, body, 0)

    @pl.when(k == 0)
    def _():
        @pl.when(i == 0)
        def _():
            issue(0, 0)

        slot = i % 2
        pltpu.make_async_copy(h_hbm.at[pl.ds(0, blk), :], xbuf.at[slot], sem.at[slot]).wait()
        xs[...] = xbuf[slot].astype(CDT)

        @pl.when(i + 1 < nb)
        def _():
            issue(i + 1, (i + 1) % 2)

        acc_ref[...] = jnp.zeros(acc_ref.shape, F32)

    @pl.when(bv_ref[i] == 1)
    def _():
        x = xs[...]
        a = _dot(x, wg_ref[...])
        act = (a * _sigmoid(a) * _dot(x, wu_ref[...])).astype(CDT)
        acc_ref[...] += _dot(act, wd_ref[...])

    @pl.when(k == nk - 1)
    def _():
        y_ref[...] = acc_ref[...]


def _moe_ffn(h, row_tok, blk_e, blk_valid, wg, wu, wd):
    t, d = h.shape
    n_rows = row_tok.shape[0]
    blk = MOE_BLK
    nb = n_rows // blk
    f = wg.shape[2]
    tf = 512
    nk = f // tf
    kk = lambda k, i, bv: jnp.where(bv[i] == 1, k, nk - 1)
    grid_spec = pltpu.PrefetchScalarGridSpec(
        num_scalar_prefetch=3,
        grid=(nb, nk),
        in_specs=[pl.BlockSpec(memory_space=pl.ANY),
                  pl.BlockSpec((None, d, tf), lambda i, k, tok, be, bv: (be[i], 0, kk(k, i, bv))),
                  pl.BlockSpec((None, d, tf), lambda i, k, tok, be, bv: (be[i], 0, kk(k, i, bv))),
                  pl.BlockSpec((None, tf, d), lambda i, k, tok, be, bv: (be[i], kk(k, i, bv), 0))],
        out_specs=pl.BlockSpec((blk, d), lambda i, k, tok, be, bv: (i, 0)),
        scratch_shapes=[pltpu.VMEM((2, blk, d), F32), pltpu.VMEM((blk, d), CDT), pltpu.VMEM((blk, d), F32),
                        pltpu.SemaphoreType.DMA((2,))],
    )
    return pl.pallas_call(
        functools.partial(_moe_body, blk=blk),
        grid_spec=grid_spec,
        out_shape=jax.ShapeDtypeStruct((n_rows, d), F32),
        compiler_params=_params(("arbitrary", "arbitrary")),
        name="moe_grouped_ffn",
    )(row_tok, blk_e, blk_valid, h, wg, wu, wd)


def _combine_body(pos_ref, y_hbm, gate_ref, x_ref, gf_ref, lg_ref, lb_ref, o_ref, ybuf, sem, *, tm):
    i = pl.program_id(0)
    nb = pl.num_programs(0)

    def issue(bi, slot):
        def body(r, _):
            for s in range(TOP_K):
                p = pos_ref[(bi * tm + r) * TOP_K + s]
                pltpu.make_async_copy(y_hbm.at[pl.ds(p, 1), :], ybuf.at[slot, s, pl.ds(r, 1), :],
                                      sem.at[slot]).start()
            return 0
        lax.fori_loop(0, tm, body, 0)

    @pl.when(i == 0)
    def _():
        issue(0, 0)

    slot = i % 2
    for s in range(TOP_K):
        pltpu.make_async_copy(y_hbm.at[pl.ds(0, tm), :], ybuf.at[slot, s], sem.at[slot]).wait()

    @pl.when(i + 1 < nb)
    def _():
        issue(i + 1, (i + 1) % 2)

    gate = gate_ref[...]
    f = ybuf[slot, 0] * gate[:, 0:1] + ybuf[slot, 1] * gate[:, 1:2]
    y = ALPHA * x_ref[...] + gf_ref[...] * f
    o_ref[...] = _layer_norm(y, lg_ref[...], lb_ref[...])


def _combine_ln(y_rows, pos, gate, x, gf, lg, lb):
    bsz, s_len, d = x.shape
    tm = 256
    nsb = s_len // tm
    grid_spec = pltpu.PrefetchScalarGridSpec(
        num_scalar_prefetch=1,
        grid=(bsz * nsb,),
        in_specs=[pl.BlockSpec(memory_space=pl.ANY),
                  pl.BlockSpec((tm, LANES), lambda i, pos: (i, 0)),
                  pl.BlockSpec((None, tm, d), lambda i, pos: (i // nsb, i % nsb, 0)),
                  pl.BlockSpec((None, 1, d), lambda i, pos: (i // nsb, 0, 0)),
                  pl.BlockSpec((1, d), lambda i, pos: (0, 0)),
                  pl.BlockSpec((1, d), lambda i, pos: (0, 0))],
        out_specs=pl.BlockSpec((None, tm, d), lambda i, pos: (i // nsb, i % nsb, 0)),
        scratch_shapes=[pltpu.VMEM((2, TOP_K, tm, d), F32), pltpu.SemaphoreType.DMA((2,))],
    )
    return pl.pallas_call(
        functools.partial(_combine_body, tm=tm),
        grid_spec=grid_spec,
        out_shape=jax.ShapeDtypeStruct(x.shape, F32),
        compiler_params=_params(("arbitrary",)),
        name="moe_combine_ln",
    )(pos, y_rows, gate, x, gf, lg, lb)


def _routing_tables(top_e, n_tok):
    blk = MOE_BLK
    flat_e = top_e.reshape(-1)
    onehot = (flat_e[:, None] == jnp.arange(N_EXPERTS, dtype=jnp.int32)[None, :]).astype(jnp.int32)
    csum = jnp.cumsum(onehot, axis=0)
    counts = csum[-1]
    rank = jnp.sum(csum * onehot, axis=1) - 1
    padded = (counts + blk - 1) // blk * blk
    pend = jnp.cumsum(padded)
    pstart = pend - padded
    pos = (jnp.sum(pstart[None, :] * onehot, axis=1) + rank).astype(jnp.int32)
    n_rows = n_tok * TOP_K + N_EXPERTS * blk
    flat_tok = jnp.repeat(jnp.arange(n_tok, dtype=jnp.int32), TOP_K)
    row_tok = jnp.zeros((n_rows,), jnp.int32).at[pos].set(flat_tok)
    blk_start = jnp.arange(n_rows // blk, dtype=jnp.int32) * blk
    blk_valid = (blk_start < pend[-1]).astype(jnp.int32)
    last_start = jnp.maximum(pend[-1] - blk, 0)
    blk_e = jnp.searchsorted(pend, jnp.minimum(blk_start, last_start), side='right').astype(jnp.int32)
    blk_e = jnp.minimum(blk_e, N_EXPERTS - 1)
    return pos, row_tok, blk_e, blk_valid


def _rope_tables(positions, rot_dim, pad_value):
    half = rot_dim // 2
    inv = THETA ** (-jnp.arange(0, rot_dim, 2, dtype=F32) / rot_dim)
    ang = positions.astype(F32)[..., None] * inv
    cos, sin = jnp.cos(ang), jnp.sin(ang)
    rest = positions.shape + (LANES - 2 * half,)
    c = jnp.concatenate([cos, cos, jnp.full(rest, pad_value, F32)], -1)
    s = jnp.concatenate([-sin, sin, jnp.zeros(rest, F32)], -1)
    return c, s


def _pad_cols(w, width):
    return jnp.pad(w, ((0, 0), (0, width - w.shape[1])))


def _mixer_weights(w_in, w_uq, w_ukv, w_a, w_b, w_c, tc):
    offs = [0]
    for n in SPLITS:
        offs.append(offs[-1] + n)
    part = lambda j: w_in[:, offs[j]:offs[j + 1]]
    d = w_in.shape[0]
    w1 = _pad_cols(jnp.concatenate([part(0), part(1), part(2)], 1), Q_RANK + KV_RANK + LANES)
    qi = jnp.pad(part(6).reshape(d, H_IDX, D_IDX), ((0, 0), (0, 0), (0, LANES - D_IDX))).reshape(d, H_IDX * LANES)
    w2 = jnp.concatenate([part(3), part(4), part(5), qi, _pad_cols(part(7), LANES), _pad_cols(part(8), LANES)], 1)
    n_cb = CONV_W // tc
    w3 = jnp.stack([part(9).reshape(d, n_cb, tc), part(10).reshape(d, n_cb, tc), part(11).reshape(d, n_cb, tc)],
                   axis=2).reshape(d, 3 * CONV_W)
    wg = part(12).reshape(d, 3, d).transpose(1, 0, 2)
    wuq = jnp.pad(w_uq.reshape(Q_RANK, H_MLA, NOPE + ROPE_MLA),
                  ((0, 0), (0, 0), (0, 2 * LANES - NOPE - ROPE_MLA))).reshape(Q_RANK, H_MLA * 2 * LANES)
    wbr = jnp.stack([w_a, w_b, w_c], 0)
    cast = lambda a: a.astype(CDT)
    return cast(w1), cast(w2), cast(w3), cast(wg), cast(wuq), cast(w_ukv), cast(wbr)


def kernel(x, c, positions, ada_w, ada_b, ln1_g, ln1_b, ln2_g, ln2_b, w_in, mla_q_norm, mla_kv_norm, w_uq, w_ukv,
           conv_w, conv_b, w_branch_a, w_branch_b, w_branch_c, w_o, ffn_w_gate, ffn_w_up, ffn_w_down, router_w,
           moe_w_gate, moe_w_up, moe_w_down):
    bsz, s_len, d = x.shape
    n_tok = bsz * s_len
    depth = ada_w.shape[0]
    conv_tc = 512

    cm, sm = _rope_tables(positions, ROPE_MLA, 0.0)
    cd, sd = _rope_tables(positions, ROT_DSA, 1.0)
    ci, si = _rope_tables(positions, ROT_IDX, 1.0)

    mod = _ada(c, ada_w, ada_b)
    vec = lambda a: a.reshape(1, d)

    h = None
    for i in range(depth):
        sh_m, sc_m, g_m, sh_f, sc_f, g_f = [mod[i, :, j * d:(j + 1) * d].reshape(bsz, 1, d) for j in range(6)]
        if h is None:
            h = _modulate(x, sc_m, sh_m, CDT)
        w1, w2, w3, wg, wuq, wukv, wbr = _mixer_weights(w_in[i], w_uq[i], w_ukv[i], w_branch_a[i], w_branch_b[i],
                                                       w_branch_c[i], conv_tc)

        qa, ka, va = _mla_prep(h, w1, mla_q_norm[i].reshape(1, -1), mla_kv_norm[i].reshape(1, -1), wuq, wukv, cm, sm)
        ya = _mla_attention(qa, ka, va)

        qd, kd, vd, qi, ki, wi = _dsa_prep(h, w2, cd, sd, ci, si)
        yb = _dsa_attention(qi, ki, wi, qd, kd, vd)

        yc = _conv_branch(h, w3, conv_w[i], conv_b[i], conv_tc)

        merged = _merge(h.reshape(n_tok, d), ya.reshape(n_tok, -1), yb.reshape(n_tok, -1), yc.reshape(n_tok, -1),
                        wg, wbr)
        moe_layer = i % 2 == 1
        x, h = _out_ln(merged.reshape(bsz, s_len, d), w_o[i].astype(CDT), x, g_m, vec(ln1_g[i]), vec(ln1_b[i]),
                       sc_f, sh_f, F32 if moe_layer else CDT)

        j = i // 2
        if i + 1 < depth:
            sh_n, sc_n = [mod[i + 1, :, q * d:(q + 1) * d].reshape(bsz, 1, d) for q in range(2)]
        else:
            sh_n, sc_n = jnp.zeros((bsz, 1, d), F32), jnp.zeros((bsz, 1, d), F32)
        if not moe_layer:
            x, h = _ffn_ln(h, ffn_w_gate[j].astype(CDT), ffn_w_up[j].astype(CDT), ffn_w_down[j].astype(CDT), x, g_f,
                           vec(ln2_g[i]), vec(ln2_b[i]), sc_n, sh_n)
        else:
            ht = h.reshape(n_tok, d)
            top_e, top_g = _router(ht, _pad_cols(router_w[j], LANES))
            pos, row_tok, blk_e, blk_valid = _routing_tables(top_e[:, :TOP_K], n_tok)
            y_rows = _moe_ffn(ht, row_tok, blk_e, blk_valid, moe_w_gate[j].astype(CDT), moe_w_up[j].astype(CDT),
                              moe_w_down[j].astype(CDT))
            x = _combine_ln(y_rows, pos, top_g, x, g_f, vec(ln2_g[i]), vec(ln2_b[i]))
            h = None if i + 1 >= depth else _modulate(x, sc_n, sh_n, CDT)
    return x
```

```python
import functools

import jax
import jax.numpy as jnp
from jax import lax
from jax.experimental import pallas as pl
from jax.experimental.pallas import tpu as pltpu

F32 = jnp.float32
CDT = jnp.bfloat16

D = 2048
DEPTH = 2
H_MLA, NOPE, ROPE_MLA, V_MLA = 8, 128, 64, 128
Q_RANK, KV_RANK = 512, 256
H_DSA, G_DSA, HD_DSA = 8, 2, 128
ROT_DSA = HD_DSA // 4
H_IDX, D_IDX = 8, 64
ROT_IDX = D_IDX // 4
TOPK_MAX = 256
CONV_W, CONV_K = 1024, 3
THETA = 500000.0
N_EXPERTS, TOP_K = 8, 2
ALPHA = (2 * DEPTH) ** 0.25
LN_EPS = 1e-5
RMS_EPS = 1e-6
SPLITS = (Q_RANK, KV_RANK, ROPE_MLA, H_DSA * HD_DSA, G_DSA * HD_DSA, G_DSA * HD_DSA,
          H_IDX * D_IDX, D_IDX, H_IDX, CONV_W, CONV_W, CONV_W, 3 * D)

LANES = 128
NEG = -1e30
LOG2E = 1.4426950408889634
VMEM_LIMIT = 56 * 1024 * 1024

MOE_BLK = 512
MLA_BLK = 512


def _params(sem, vmem=VMEM_LIMIT):
    return pltpu.CompilerParams(dimension_semantics=sem, vmem_limit_bytes=vmem)


def _sigmoid(x):
    return 1.0 / (1.0 + jnp.exp(-x))


def _dot(a, b):
    return jnp.dot(a, b, preferred_element_type=F32)


def _dot_t(a, b):
    return lax.dot_general(a, b, (((1,), (1,)), ((), ())), preferred_element_type=F32)


def _layer_norm(y, g, b):
    mu = jnp.mean(y, -1, keepdims=True)
    d = y - mu
    var = jnp.mean(d * d, -1, keepdims=True)
    return d * lax.rsqrt(var + LN_EPS) * g + b


def _rope(y, c, s, half):
    lane = lax.broadcasted_iota(jnp.int32, y.shape, 1)
    swapped = jnp.where(lane < half, pltpu.roll(y, LANES - half, 1), pltpu.roll(y, half, 1))
    return y * c + swapped * s


def _ada_body(c_ref, w_ref, b_ref, o_ref):
    c = c_ref[...]
    ca = (c * _sigmoid(c)).astype(CDT)
    o_ref[...] = _dot(ca, w_ref[...].astype(CDT)) + b_ref[...]


def _ada(c, ada_w, ada_b):
    depth, d, n = ada_w.shape
    bsz = c.shape[0]
    rows = 8
    cp = jnp.zeros((rows, d), F32).at[:bsz].set(c)
    tn = 1024
    out = pl.pallas_call(
        _ada_body,
        grid=(depth, n // tn),
        in_specs=[pl.BlockSpec((rows, d), lambda l, j: (0, 0)),
                  pl.BlockSpec((None, d, tn), lambda l, j: (l, 0, j)),
                  pl.BlockSpec((None, 1, tn), lambda l, j: (l, 0, j))],
        out_specs=pl.BlockSpec((None, rows, tn), lambda l, j: (l, 0, j)),
        out_shape=jax.ShapeDtypeStruct((depth, rows, n), F32),
        compiler_params=_params(("arbitrary", "arbitrary")),
        name="ada_mod",
    )(cp, ada_w, ada_b.reshape(depth, 1, n))
    return out[:, :bsz]


def _mod_body(x_ref, sc_ref, sh_ref, o_ref):
    o_ref[...] = (x_ref[...] * (1.0 + sc_ref[...]) + sh_ref[...]).astype(o_ref.dtype)


def _modulate(x, sc, sh, out_dtype):
    bsz, s_len, d = x.shape
    tm = 512
    vec = pl.BlockSpec((None, 1, d), lambda b, i: (b, 0, 0))
    return pl.pallas_call(
        _mod_body,
        grid=(bsz, s_len // tm),
        in_specs=[pl.BlockSpec((None, tm, d), lambda b, i: (b, i, 0)), vec, vec],
        out_specs=pl.BlockSpec((None, tm, d), lambda b, i: (b, i, 0)),
        out_shape=jax.ShapeDtypeStruct(x.shape, out_dtype),
        compiler_params=_params(("arbitrary", "arbitrary")),
        name="modulate",
    )(x, sc, sh)


def _mla_prep_body(h_ref, w1_ref, qg_ref, kg_ref, wuq_ref, wukv_ref, c_ref, s_ref, q_ref, k_ref, v_ref):
    z = _dot(h_ref[...], w1_ref[...])
    cq = z[:, :Q_RANK]
    ckv = z[:, Q_RANK:Q_RANK + KV_RANK]
    kr = z[:, Q_RANK + KV_RANK:]
    nq = (cq * lax.rsqrt(jnp.mean(cq * cq, -1, keepdims=True) + RMS_EPS) * qg_ref[...]).astype(CDT)
    nkv = (ckv * lax.rsqrt(jnp.mean(ckv * ckv, -1, keepdims=True) + RMS_EPS) * kg_ref[...]).astype(CDT)
    qa = _dot(nq, wuq_ref[...])
    kva = _dot(nkv, wukv_ref[...])
    c = c_ref[...]
    s = s_ref[...]
    scale = (NOPE + ROPE_MLA) ** -0.5 * LOG2E
    krot_t = _rope(kr, c, s, ROPE_MLA // 2).T.astype(CDT)
    lane = lax.broadcasted_iota(jnp.int32, (h_ref.shape[0], LANES), 1)
    ones_col = jnp.where(lane == 0, 1.0, 0.0).astype(CDT)
    for h in range(H_MLA):
        lo = h * 2 * LANES
        q_ref[h, :, 0:LANES] = (qa[:, lo:lo + LANES] * scale).astype(CDT)
        q_ref[h, :, LANES:2 * LANES] = (_rope(qa[:, lo + LANES:lo + 2 * LANES], c, s, ROPE_MLA // 2)
                                        * scale).astype(CDT)
        k_ref[h, 0:LANES, :] = kva[:, lo:lo + LANES].T.astype(CDT)
        k_ref[h, LANES:2 * LANES, :] = krot_t
        v_ref[h, :, 0:LANES] = kva[:, lo + LANES:lo + 2 * LANES].astype(CDT)
        v_ref[h, :, LANES:2 * LANES] = ones_col


def _mla_prep(h, w1, qg, kg, wuq, wukv, ctab, stab):
    bsz, s_len, d = h.shape
    tm = min(MLA_BLK, s_len)
    full = lambda a: pl.BlockSpec(a.shape, lambda b, i: (0,) * a.ndim)
    row = lambda w: pl.BlockSpec((None, tm, w), lambda b, i: (b, i, 0))
    head = lambda w: pl.BlockSpec((None, H_MLA, tm, w), lambda b, i: (b, 0, i, 0))
    return pl.pallas_call(
        _mla_prep_body,
        grid=(bsz, s_len // tm),
        in_specs=[row(d), full(w1), full(qg), full(kg), full(wuq), full(wukv), row(LANES), row(LANES)],
        out_specs=[head(2 * LANES),
                   pl.BlockSpec((None, H_MLA, None, 2 * LANES, tm), lambda b, i: (b, 0, i, 0, 0)),
                   head(2 * LANES)],
        out_shape=[jax.ShapeDtypeStruct((bsz, H_MLA, s_len, 2 * LANES), CDT),
                   jax.ShapeDtypeStruct((bsz, H_MLA, s_len // tm, 2 * LANES, tm), CDT),
                   jax.ShapeDtypeStruct((bsz, H_MLA, s_len, 2 * LANES), CDT)],
        compiler_params=_params(("arbitrary", "arbitrary")),
        name="mla_prep",
    )(h, w1, qg, kg, wuq, wukv, ctab, stab)


def _flash_body(q_ref, k_ref, v_ref, o_ref, s_ref, mx_ref, m_ref, acc_ref, *, blk):
    qi = pl.program_id(2)
    dv = V_MLA

    def scores(hd, j, masked):
        s = _dot(q_ref[hd], k_ref[hd, j])
        if masked:
            row = lax.broadcasted_iota(jnp.int32, s.shape, 0)
            col = lax.broadcasted_iota(jnp.int32, s.shape, 1)
            s = jnp.where(col <= row, s, NEG)
        return s, jnp.max(s, -1, keepdims=True)

    def put(hd, sv):
        s_ref[hd], mx_ref[hd] = sv

    def accumulate(hd, j):
        off = pl.multiple_of(j * blk, blk)
        m_old = m_ref[hd]
        m_new = jnp.maximum(m_old, mx_ref[hd])
        alpha = jnp.exp2(m_old - m_new)
        p = jnp.exp2((s_ref[hd] - m_new).astype(CDT))
        acc_ref[hd] = alpha * acc_ref[hd] + _dot(p, v_ref[hd, pl.ds(off, blk), :])
        m_ref[hd] = m_new

    m_ref[...] = jnp.full(m_ref.shape, NEG, F32)
    acc_ref[...] = jnp.zeros(acc_ref.shape, F32)
    put(0, scores(0, qi, True))
    put(1, scores(1, qi, True))

    def body(n, _):
        j_prev = jnp.where(n == 1, qi, n - 2)
        accumulate(0, j_prev)
        accumulate(1, j_prev)
        put(0, scores(0, n - 1, False))
        put(1, scores(1, n - 1, False))
        return 0

    lax.fori_loop(1, qi + 1, body, 0)
    j_last = jnp.where(qi == 0, qi, qi - 1)
    accumulate(0, j_last)
    accumulate(1, j_last)
    for hd in range(2):
        acc = acc_ref[hd]
        o_ref[:, hd * dv:(hd + 1) * dv] = (acc[:, :dv] / acc[:, dv:dv + 1]).astype(o_ref.dtype)


def _mla_attention(q, k, v):
    bsz, nh, s_len, dq = q.shape
    dve = v.shape[-1]
    dv = V_MLA
    blk = min(MLA_BLK, s_len)
    return pl.pallas_call(
        functools.partial(_flash_body, blk=blk),
        grid=(bsz, nh // 2, s_len // blk),
        in_specs=[pl.BlockSpec((None, 2, blk, dq), lambda b, h, i: (b, h, i, 0)),
                  pl.BlockSpec((None, 2, s_len // blk, dq, blk), lambda b, h, i: (b, h, 0, 0, 0)),
                  pl.BlockSpec((None, 2, s_len, dve), lambda b, h, i: (b, h, 0, 0))],
        out_specs=pl.BlockSpec((None, blk, 2 * dv), lambda b, h, i: (b, i, h)),
        out_shape=jax.ShapeDtypeStruct((bsz, s_len, nh * dv), CDT),
        scratch_shapes=[pltpu.VMEM((2, blk, blk), F32), pltpu.VMEM((2, blk, 1), F32),
                        pltpu.VMEM((2, blk, 1), F32), pltpu.VMEM((2, blk, dve), F32)],
        compiler_params=_params(("arbitrary", "arbitrary", "arbitrary")),
        name="mla_attention",
    )(q, k, v)


DSA_CHUNK = 512
VT_PAD = 16
VT_ROWS = HD_DSA + VT_PAD


def _dsa_prep_body(h_ref, w2_ref, cd_ref, sd_ref, ci_ref, si_ref, q_ref, k_ref, vt_ref, qi_ref, ki_ref, wit_ref):
    z = _dot(h_ref[...], w2_ref[...])
    cd, sd, ci, si = cd_ref[...], sd_ref[...], ci_ref[...], si_ref[...]
    col = lambda j: z[:, j * LANES:(j + 1) * LANES]
    for h in range(H_DSA):
        q_ref[h] = (_rope(col(h), cd, sd, ROT_DSA // 2) * (HD_DSA ** -0.5 * LOG2E)).T.astype(CDT)
    for g in range(G_DSA):
        k_ref[:, g * LANES:(g + 1) * LANES] = _rope(col(H_DSA + g), cd, sd, ROT_DSA // 2).astype(CDT)
    base = H_DSA + G_DSA
    sub = lax.broadcasted_iota(jnp.int32, (VT_PAD, z.shape[0]), 0)
    ones_row = jnp.where(sub == 0, 1.0, 0.0).astype(CDT)
    for g in range(G_DSA):
        vt_ref[g * VT_ROWS:g * VT_ROWS + LANES, :] = col(base + g).T.astype(CDT)
        vt_ref[g * VT_ROWS + LANES:(g + 1) * VT_ROWS, :] = ones_row
    base += G_DSA
    for h in range(H_IDX):
        qi_ref[h] = (_rope(col(base + h), ci, si, ROT_IDX // 2) * D_IDX ** -0.5).T.astype(CDT)
    base += H_IDX
    ki_ref[...] = _rope(col(base), ci, si, ROT_IDX // 2).astype(CDT)
    wit_ref[...] = col(base + 1).T[0:H_IDX, :] * H_IDX ** -0.5


def _dsa_prep(h, w2, cd, sd, ci, si):
    bsz, s_len, d = h.shape
    tm = min(DSA_CHUNK, s_len)
    row = lambda w: pl.BlockSpec((None, tm, w), lambda b, i: (b, i, 0))
    return pl.pallas_call(
        _dsa_prep_body,
        grid=(bsz, s_len // tm),
        in_specs=[row(d), pl.BlockSpec(w2.shape, lambda b, i: (0, 0)), row(LANES), row(LANES), row(LANES),
                  row(LANES)],
        out_specs=[pl.BlockSpec((None, H_DSA, LANES, tm), lambda b, i: (b, 0, 0, i)), row(G_DSA * LANES),
                   pl.BlockSpec((None, None, G_DSA * VT_ROWS, tm), lambda b, i: (b, i, 0, 0)),
                   pl.BlockSpec((None, H_IDX, LANES, tm), lambda b, i: (b, 0, 0, i)), row(LANES),
                   pl.BlockSpec((None, H_IDX, tm), lambda b, i: (b, 0, i))],
        out_shape=[jax.ShapeDtypeStruct((bsz, H_DSA, LANES, s_len), CDT),
                   jax.ShapeDtypeStruct((bsz, s_len, G_DSA * LANES), CDT),
                   jax.ShapeDtypeStruct((bsz, s_len // tm, G_DSA * VT_ROWS, tm), CDT),
                   jax.ShapeDtypeStruct((bsz, H_IDX, LANES, s_len), CDT),
                   jax.ShapeDtypeStruct((bsz, s_len, LANES), CDT),
                   jax.ShapeDtypeStruct((bsz, H_IDX, s_len), F32)],
        compiler_params=_params(("arbitrary", "arbitrary")),
        name="dsa_prep",
    )(h, w2, cd, sd, ci, si)


FOLD_ROWS = 64


def _fold(x):
    return jnp.sum(x.reshape(x.shape[0] // FOLD_ROWS, FOLD_ROWS, x.shape[1]), axis=0)


def _dsa_body(qi_ref, ki_ref, wit_ref, q_ref, k_ref, vt_ref, tri_ref, o_ref,
              key_ref, sel_ref, qia_ref, qg_ref, s_ref, mx_ref, m_ref, acc_ref, *, tq, tkc, n_sel):
    i = pl.program_id(1)
    q0 = i * tq
    nkc = (q0 + tq + tkc - 1) // tkc
    n_rep = H_DSA // G_DSA
    int_min = jnp.iinfo(jnp.int32).min
    key_neg_inf = jnp.int32(-8388608) ^ jnp.int32(0x7FFFFFFF)
    qpos = q0 + lax.broadcasted_iota(jnp.int32, (tkc, tq), 1)

    for h in range(H_IDX):
        qia_ref[:, h * tq:(h + 1) * tq] = qi_ref[h]
    for g in range(G_DSA):
        for r in range(n_rep):
            qg_ref[g, :, r * tq:(r + 1) * tq] = q_ref[g * n_rep + r]

    def score_chunk(c, _):
        off = pl.multiple_of(c * tkc, tkc)
        kc = ki_ref[pl.ds(off, tkc), :]
        w = wit_ref[...]
        lg = _dot(kc, qia_ref[...])
        sc = jnp.zeros((tkc, tq), F32)
        for h in range(H_IDX):
            sc = sc + w[h:h + 1, :] * jnp.maximum(lg[:, h * tq:(h + 1) * tq], 0.0)
        kpos = off + lax.broadcasted_iota(jnp.int32, (tkc, tq), 0)
        sc = jnp.where(kpos <= qpos, sc, -jnp.inf)
        bits = pltpu.bitcast(sc, jnp.int32)
        key_ref[c] = jnp.where(bits >= 0, bits, bits ^ jnp.int32(0x7FFFFFFF))
        return 0

    lax.fori_loop(0, nkc, score_chunk, 0)

    def count(pred):
        def body(c, cnt):
            return cnt + _fold(jnp.where(pred(key_ref[c]), 1.0, 0.0))
        part = lax.fori_loop(0, nkc, body, jnp.zeros((FOLD_ROWS, tq), F32))
        return jnp.sum(part, axis=0, keepdims=True)

    def bit_step(bi, thr):
        cand = thr + lax.shift_left(jnp.int32(1), 31 - bi)
        cnt = count(lambda kk: kk >= cand)
        return jnp.where(cnt >= n_sel, cand, thr)

    thr = lax.fori_loop(0, 32, bit_step, jnp.full((1, tq), int_min, jnp.int32))

    cnt_gt = count(lambda kk: kk > thr)
    cnt_eq = count(lambda kk: kk == thr)
    need = n_sel - cnt_gt
    excess = jnp.where(thr > key_neg_inf, cnt_eq - need, 0.0)

    def mask_plain():
        def body(c, _):
            kk = key_ref[c]
            sel_ref[c] = jnp.where(kk >= thr, jnp.where(kk > key_neg_inf, 0.0, NEG), NEG)
            return 0
        lax.fori_loop(0, nkc, body, 0)

    def mask_ranked():
        def body(c, seen):
            kk = key_ref[c]
            tie = kk == thr
            pre = _dot(tri_ref[...], jnp.where(tie, 1.0, 0.0).astype(CDT))
            rank = seen + pre
            keep_tie = jnp.where(tie, jnp.where(rank <= need, 0.0, NEG), NEG)
            keep = jnp.where(kk > thr, 0.0, keep_tie)
            sel_ref[c] = jnp.where(kk > key_neg_inf, keep, NEG)
            return seen + pre[tkc - 1:tkc, :]
        lax.fori_loop(0, nkc, body, jnp.zeros((1, tq), F32))

    lax.cond(jnp.max(excess) > 0.0, mask_ranked, mask_plain)

    m_ref[...] = jnp.full(m_ref.shape, NEG, F32)
    acc_ref[...] = jnp.zeros(acc_ref.shape, F32)

    def scores(g, c):
        off = pl.multiple_of(c * tkc, tkc)
        msk = sel_ref[c]
        msk = jnp.concatenate([msk] * n_rep, axis=1)
        kg = k_ref[pl.ds(off, tkc), g * LANES:(g + 1) * LANES]
        s = _dot(kg, qg_ref[g]) + msk
        return s, jnp.max(s, axis=0, keepdims=True)

    def put(g, sv):
        s_ref[g], mx_ref[g] = sv

    def accumulate(g, c):
        m_old = m_ref[g]
        m_new = jnp.maximum(m_old, mx_ref[g])
        alpha = jnp.exp2(m_old - m_new)
        p = jnp.exp2((s_ref[g] - m_new).astype(CDT))
        acc_ref[g] = alpha * acc_ref[g] + _dot(vt_ref[c, g * VT_ROWS:(g + 1) * VT_ROWS, :], p)
        m_ref[g] = m_new

    put(0, scores(0, 0))
    put(1, scores(1, 0))

    def att_chunk(c, _):
        sv0 = scores(0, c)
        sv1 = scores(1, c)
        accumulate(0, c - 1)
        accumulate(1, c - 1)
        put(0, sv0)
        put(1, sv1)
        return 0

    lax.fori_loop(1, nkc, att_chunk, 0)
    accumulate(0, nkc - 1)
    accumulate(1, nkc - 1)
    for g in range(G_DSA):
        acc = acc_ref[g]
        out = acc[0:HD_DSA, :] / acc[HD_DSA:HD_DSA + 1, :]
        for r in range(n_rep):
            hh = g * n_rep + r
            o_ref[:, hh * LANES:(hh + 1) * LANES] = out[:, r * tq:(r + 1) * tq].T.astype(o_ref.dtype)


def _dsa_attention(qi, ki, wit, q, k, vt):
    bsz, s_len, _ = k.shape
    tq = 128
    tkc = min(DSA_CHUNK, s_len)
    n_chunks = s_len // tkc
    n_sel = min(TOPK_MAX, s_len // 4)
    n_rep = H_DSA // G_DSA
    tri = (lax.broadcasted_iota(jnp.int32, (tkc, tkc), 1)
           <= lax.broadcasted_iota(jnp.int32, (tkc, tkc), 0)).astype(CDT)
    row = lambda w: pl.BlockSpec((None, tq, w), lambda b, i: (b, i, 0))
    seq = lambda w: pl.BlockSpec((None, s_len, w), lambda b, i: (b, 0, 0))
    return pl.pallas_call(
        functools.partial(_dsa_body, tq=tq, tkc=tkc, n_sel=n_sel),
        grid=(bsz, s_len // tq),
        in_specs=[pl.BlockSpec((None, H_IDX, LANES, tq), lambda b, i: (b, 0, 0, i)), seq(LANES),
                  pl.BlockSpec((None, H_IDX, tq), lambda b, i: (b, 0, i)),
                  pl.BlockSpec((None, H_DSA, LANES, tq), lambda b, i: (b, 0, 0, i)), seq(G_DSA * LANES),
                  pl.BlockSpec((None, n_chunks, G_DSA * VT_ROWS, tkc), lambda b, i: (b, 0, 0, 0)),
                  pl.BlockSpec((tkc, tkc), lambda b, i: (0, 0))],
        out_specs=row(H_DSA * LANES),
        out_shape=jax.ShapeDtypeStruct((bsz, s_len, H_DSA * LANES), CDT),
        scratch_shapes=[pltpu.VMEM((n_chunks, tkc, tq), jnp.int32),
                        pltpu.VMEM((n_chunks, tkc, tq), F32),
                        pltpu.VMEM((LANES, H_IDX * tq), CDT),
                        pltpu.VMEM((G_DSA, LANES, n_rep * tq), CDT),
                        pltpu.VMEM((G_DSA, tkc, n_rep * tq), F32),
                        pltpu.VMEM((G_DSA, 1, n_rep * tq), F32),
                        pltpu.VMEM((G_DSA, 1, n_rep * tq), F32),
                        pltpu.VMEM((G_DSA, VT_ROWS, n_rep * tq), F32)],
        compiler_params=_params(("arbitrary", "arbitrary")),
        name="dsa_attention",
    )(qi, ki, wit, q, k, vt, tri)


def _conv_body(h_ref, w_ref, cw_ref, cb_ref, o_ref, tail_ref, *, tm, tc):
    @pl.when(pl.program_id(2) == 0)
    def _():
        tail_ref[...] = jnp.zeros(tail_ref.shape, F32)

    z = _dot(h_ref[...], w_ref[...])
    gb = z[:, :tc]
    u = z[:, tc:2 * tc] * z[:, 2 * tc:]
    prev = tail_ref[...]
    row = lax.broadcasted_iota(jnp.int32, (tm, tc), 0)
    u1 = jnp.where(row == 0, prev[7:8, :], pltpu.roll(u, 1, 0))
    u2 = jnp.where(row == 0, prev[6:7, :], jnp.where(row == 1, prev[7:8, :], pltpu.roll(u, 2, 0)))
    cw = cw_ref[...]
    y = cw[0:1, :] * u2 + cw[1:2, :] * u1 + cw[2:3, :] * u + cb_ref[...]
    o_ref[...] = (gb * y).astype(o_ref.dtype)
    tail_ref[...] = u[tm - 8:tm, :]


def _conv_branch(h, w3, conv_w, conv_b, tc):
    bsz, s_len, d = h.shape
    tm = 512
    n_cb = CONV_W // tc
    return pl.pallas_call(
        functools.partial(_conv_body, tm=tm, tc=tc),
        grid=(n_cb, bsz, s_len // tm),
        in_specs=[pl.BlockSpec((None, tm, d), lambda cb, b, i: (b, i, 0)),
                  pl.BlockSpec((d, 3 * tc), lambda cb, b, i: (0, cb)),
                  pl.BlockSpec((CONV_K, tc), lambda cb, b, i: (0, cb)),
                  pl.BlockSpec((1, tc), lambda cb, b, i: (0, cb))],
        out_specs=pl.BlockSpec((None, tm, tc), lambda cb, b, i: (b, i, cb)),
        out_shape=jax.ShapeDtypeStruct((bsz, s_len, CONV_W), CDT),
        scratch_shapes=[pltpu.VMEM((8, tc), F32)],
        compiler_params=_params(("arbitrary", "arbitrary", "arbitrary")),
        name="conv_branch",
    )(h, w3, conv_w, conv_b.reshape(1, CONV_W))


def _merge_body(h_ref, ya_ref, yb_ref, yc_ref, wg_ref, wbr_ref, o_ref):
    h = h_ref[...]
    acc = None
    for j, y_ref in enumerate((ya_ref, yb_ref, yc_ref)):
        term = _sigmoid(_dot(h, wg_ref[j])) * _dot(y_ref[...], wbr_ref[j])
        acc = term if acc is None else acc + term
    o_ref[...] = acc.astype(o_ref.dtype)


def _merge(h, ya, yb, yc, wg, wbr):
    t, d = h.shape
    tm, tn = 512, 512
    yw = ya.shape[1]
    row = lambda w: pl.BlockSpec((tm, w), lambda j, i: (i, 0))
    return pl.pallas_call(
        _merge_body,
        grid=(d // tn, t // tm),
        in_specs=[row(d), row(yw), row(yw), row(yw),
                  pl.BlockSpec((3, d, tn), lambda j, i: (0, 0, j)),
                  pl.BlockSpec((3, yw, tn), lambda j, i: (0, 0, j))],
        out_specs=pl.BlockSpec((tm, tn), lambda j, i: (i, j)),
        out_shape=jax.ShapeDtypeStruct((t, d), CDT),
        compiler_params=_params(("arbitrary", "arbitrary")),
        name="gated_merge",
    )(h, ya, yb, yc, wg, wbr)


def _out_ln_body(m_ref, wo_ref, x_ref, gm_ref, lg_ref, lb_ref, sc_ref, sh_ref, x1_ref, h_ref):
    y = ALPHA * x_ref[...] + gm_ref[...] * _dot(m_ref[...], wo_ref[...])
    x1 = _layer_norm(y, lg_ref[...], lb_ref[...])
    x1_ref[...] = x1
    h_ref[...] = (x1 * (1.0 + sc_ref[...]) + sh_ref[...]).astype(h_ref.dtype)


def _out_ln(merged, wo, x, gm, lg, lb, sc, sh, h_dtype):
    bsz, s_len, d = x.shape
    tm = 256
    row = pl.BlockSpec((None, tm, d), lambda b, i: (b, i, 0))
    per_b = pl.BlockSpec((None, 1, d), lambda b, i: (b, 0, 0))
    vec = pl.BlockSpec((1, d), lambda b, i: (0, 0))
    return pl.pallas_call(
        _out_ln_body,
        grid=(bsz, s_len // tm),
        in_specs=[row, pl.BlockSpec((d, d), lambda b, i: (0, 0)), row, per_b, vec, vec, per_b, per_b],
        out_specs=[row, row],
        out_shape=[jax.ShapeDtypeStruct(x.shape, F32), jax.ShapeDtypeStruct(x.shape, h_dtype)],
        compiler_params=_params(("arbitrary", "arbitrary")),
        name="out_proj_ln",
    )(merged, wo, x, gm, lg, lb, sc, sh)


def _ffn_body(h_ref, wg_ref, wu_ref, wd_ref, x_ref, gf_ref, lg_ref, lb_ref, sc_ref, sh_ref, x2_ref, h2_ref,
              acc_ref):
    k = pl.program_id(2)

    @pl.when(k == 0)
    def _():
        acc_ref[...] = jnp.zeros(acc_ref.shape, F32)

    h = h_ref[...]
    a = _dot(h, wg_ref[...])
    act = (a * _sigmoid(a) * _dot(h, wu_ref[...])).astype(CDT)
    acc_ref[...] += _dot(act, wd_ref[...])

    @pl.when(k == pl.num_programs(2) - 1)
    def _():
        y = ALPHA * x_ref[...] + gf_ref[...] * acc_ref[...]
        x2 = _layer_norm(y, lg_ref[...], lb_ref[...])
        x2_ref[...] = x2
        h2_ref[...] = (x2 * (1.0 + sc_ref[...]) + sh_ref[...]).astype(h2_ref.dtype)


def _ffn_ln(h, wg, wu, wd, x, gf, lg, lb, sc, sh):
    bsz, s_len, d = x.shape
    f = wg.shape[1]
    tm, tf = 512, 512
    row = pl.BlockSpec((None, tm, d), lambda b, i, k: (b, i, 0))
    per_b = pl.BlockSpec((None, 1, d), lambda b, i, k: (b, 0, 0))
    vec = pl.BlockSpec((1, d), lambda b, i, k: (0, 0))
    return pl.pallas_call(
        _ffn_body,
        grid=(bsz, s_len // tm, f // tf),
        in_specs=[row, pl.BlockSpec((d, tf), lambda b, i, k: (0, k)), pl.BlockSpec((d, tf), lambda b, i, k: (0, k)),
                  pl.BlockSpec((tf, d), lambda b, i, k: (k, 0)), row, per_b, vec, vec, per_b, per_b],
        out_specs=[row, row],
        out_shape=[jax.ShapeDtypeStruct(x.shape, F32), jax.ShapeDtypeStruct(x.shape, CDT)],
        scratch_shapes=[pltpu.VMEM((tm, d), F32)],
        compiler_params=_params(("arbitrary", "arbitrary", "arbitrary")),
        name="ffn_ln",
    )(h, wg, wu, wd, x, gf, lg, lb, sc, sh)


def _router_body(h_ref, wr_ref, e_ref, g_ref):
    lg = jnp.dot(h_ref[...], wr_ref[...], preferred_element_type=F32, precision=lax.Precision.HIGHEST)
    lane = lax.broadcasted_iota(jnp.int32, lg.shape, 1)
    lanef = lane.astype(F32)
    lg = jnp.where(lane < N_EXPERTS, lg, -jnp.inf)
    m1 = jnp.max(lg, -1, keepdims=True)
    i1 = jnp.min(jnp.where(lg == m1, lanef, float(LANES)), -1, keepdims=True)
    lg2 = jnp.where(lanef == i1, -jnp.inf, lg)
    m2 = jnp.max(lg2, -1, keepdims=True)
    i2 = jnp.min(jnp.where(lg2 == m2, lanef, float(LANES)), -1, keepdims=True)
    e = jnp.exp(m2 - m1)
    g1 = 1.0 / (1.0 + e)
    g2 = e / (1.0 + e)
    e_ref[...] = jnp.where(lane == 0, i1, jnp.where(lane == 1, i2, 0.0)).astype(jnp.int32)
    g_ref[...] = jnp.where(lane == 0, g1, jnp.where(lane == 1, g2, 0.0))


def _router(h, wr):
    t, d = h.shape
    tm = 512
    return pl.pallas_call(
        _router_body,
        grid=(t // tm,),
        in_specs=[pl.BlockSpec((tm, d), lambda i: (i, 0)), pl.BlockSpec((d, LANES), lambda i: (0, 0))],
        out_specs=[pl.BlockSpec((tm, LANES), lambda i: (i, 0)), pl.BlockSpec((tm, LANES), lambda i: (i, 0))],
        out_shape=[jax.ShapeDtypeStruct((t, LANES), jnp.int32), jax.ShapeDtypeStruct((t, LANES), F32)],
        compiler_params=_params(("arbitrary",)),
        name="moe_router",
    )(h, wr)


def _moe_body(tok_ref, be_ref, bv_ref, h_hbm, wg_ref, wu_ref, wd_ref, y_ref, xbuf, xs, acc_ref, sem, *, blk):
    i = pl.program_id(0)
    k = pl.program_id(1)
    nb = pl.num_programs(0)
    nk = pl.num_programs(1)

    def row_copy(tok, slot, r):
        return pltpu.make_async_copy(h_hbm.at[pl.ds(tok, 1), :], xbuf.at[slot, pl.ds(r, 1), :], sem.at[slot])

    def issue(bi, slot):
        def body(r, _):
            row_copy(tok_ref[bi * blk + r], slot, r).start()
            return 0
        lax.fori_loop(0, blk, body, 0)

    @pl.when(k == 0)
    def _():
        @pl.when(i == 0)
        def _():
            issue(0, 0)

        slot = i % 2
        pltpu.make_async_copy(h_hbm.at[pl.ds(0, blk), :], xbuf.at[slot], sem.at[slot]).wait()
        xs[...] = xbuf[slot].astype(CDT)

        @pl.when(i + 1 < nb)
        def _():
            issue(i + 1, (i + 1) % 2)

        acc_ref[...] = jnp.zeros(acc_ref.shape, F32)

    @pl.when(bv_ref[i] == 1)
    def _():
        x = xs[...]
        a = _dot(x, wg_ref[...])
        act = (a * _sigmoid(a) * _dot(x, wu_ref[...])).astype(CDT)
        acc_ref[...] += _dot(act, wd_ref[...])

    @pl.when(k == nk - 1)
    def _():
        y_ref[...] = acc_ref[...]


def _moe_ffn(h, row_tok, blk_e, blk_valid, wg, wu, wd):
    t, d = h.shape
    n_rows = row_tok.shape[0]
    blk = MOE_BLK
    nb = n_rows // blk
    f = wg.shape[2]
    tf = 512
    nk = f // tf
    kk = lambda k, i, bv: jnp.where(bv[i] == 1, k, nk - 1)
    grid_spec = pltpu.PrefetchScalarGridSpec(
        num_scalar_prefetch=3,
        grid=(nb, nk),
        in_specs=[pl.BlockSpec(memory_space=pl.ANY),
                  pl.BlockSpec((None, d, tf), lambda i, k, tok, be, bv: (be[i], 0, kk(k, i, bv))),
                  pl.BlockSpec((None, d, tf), lambda i, k, tok, be, bv: (be[i], 0, kk(k, i, bv))),
                  pl.BlockSpec((None, tf, d), lambda i, k, tok, be, bv: (be[i], kk(k, i, bv), 0))],
        out_specs=pl.BlockSpec((blk, d), lambda i, k, tok, be, bv: (i, 0)),
        scratch_shapes=[pltpu.VMEM((2, blk, d), F32), pltpu.VMEM((blk, d), CDT), pltpu.VMEM((blk, d), F32),
                        pltpu.SemaphoreType.DMA((2,))],
    )
    return pl.pallas_call(
        functools.partial(_moe_body, blk=blk),
        grid_spec=grid_spec,
        out_shape=jax.ShapeDtypeStruct((n_rows, d), F32),
        compiler_params=_params(("arbitrary", "arbitrary")),
        name="moe_grouped_ffn",
    )(row_tok, blk_e, blk_valid, h, wg, wu, wd)


def _combine_body(pos_ref, y_hbm, gate_ref, x_ref, gf_ref, lg_ref, lb_ref, o_ref, ybuf, sem, *, tm):
    i = pl.program_id(0)
    nb = pl.num_programs(0)

    def issue(bi, slot):
        def body(r, _):
            for s in range(TOP_K):
                p = pos_ref[(bi * tm + r) * TOP_K + s]
                pltpu.make_async_copy(y_hbm.at[pl.ds(p, 1), :], ybuf.at[slot, s, pl.ds(r, 1), :],
                                      sem.at[slot]).start()
            return 0
        lax.fori_loop(0, tm, body, 0)

    @pl.when(i == 0)
    def _():
        issue(0, 0)

    slot = i % 2
    for s in range(TOP_K):
        pltpu.make_async_copy(y_hbm.at[pl.ds(0, tm), :], ybuf.at[slot, s], sem.at[slot]).wait()

    @pl.when(i + 1 < nb)
    def _():
        issue(i + 1, (i + 1) % 2)

    gate = gate_ref[...]
    f = ybuf[slot, 0] * gate[:, 0:1] + ybuf[slot, 1] * gate[:, 1:2]
    y = ALPHA * x_ref[...] + gf_ref[...] * f
    o_ref[...] = _layer_norm(y, lg_ref[...], lb_ref[...])


def _combine_ln(y_rows, pos, gate, x, gf, lg, lb):
    bsz, s_len, d = x.shape
    tm = 256
    nsb = s_len // tm
    grid_spec = pltpu.PrefetchScalarGridSpec(
        num_scalar_prefetch=1,
        grid=(bsz * nsb,),
        in_specs=[pl.BlockSpec(memory_space=pl.ANY),
                  pl.BlockSpec((tm, LANES), lambda i, pos: (i, 0)),
                  pl.BlockSpec((None, tm, d), lambda i, pos: (i // nsb, i % nsb, 0)),
                  pl.BlockSpec((None, 1, d), lambda i, pos: (i // nsb, 0, 0)),
                  pl.BlockSpec((1, d), lambda i, pos: (0, 0)),
                  pl.BlockSpec((1, d), lambda i, pos: (0, 0))],
        out_specs=pl.BlockSpec((None, tm, d), lambda i, pos: (i // nsb, i % nsb, 0)),
        scratch_shapes=[pltpu.VMEM((2, TOP_K, tm, d), F32), pltpu.SemaphoreType.DMA((2,))],
    )
    return pl.pallas_call(
        functools.partial(_combine_body, tm=tm),
        grid_spec=grid_spec,
        out_shape=jax.ShapeDtypeStruct(x.shape, F32),
        compiler_params=_params(("arbitrary",)),
        name="moe_combine_ln",
    )(pos, y_rows, gate, x, gf, lg, lb)


def _routing_tables(top_e, n_tok):
    blk = MOE_BLK
    flat_e = top_e.reshape(-1)
    onehot = (flat_e[:, None] == jnp.arange(N_EXPERTS, dtype=jnp.int32)[None, :]).astype(jnp.int32)
    csum = jnp.cumsum(onehot, axis=0)
    counts = csum[-1]
    rank = jnp.sum(csum * onehot, axis=1) - 1
    padded = (counts + blk - 1) // blk * blk
    pend = jnp.cumsum(padded)
    pstart = pend - padded
    pos = (jnp.sum(pstart[None, :] * onehot, axis=1) + rank).astype(jnp.int32)
    n_rows = n_tok * TOP_K + N_EXPERTS * blk
    flat_tok = jnp.repeat(jnp.arange(n_tok, dtype=jnp.int32), TOP_K)
    row_tok = jnp.zeros((n_rows,), jnp.int32).at[pos].set(flat_tok)
    blk_start = jnp.arange(n_rows // blk, dtype=jnp.int32) * blk
    blk_valid = (blk_start < pend[-1]).astype(jnp.int32)
    last_start = jnp.maximum(pend[-1] - blk, 0)
    blk_e = jnp.searchsorted(pend, jnp.minimum(blk_start, last_start), side='right').astype(jnp.int32)
    blk_e = jnp.minimum(blk_e, N_EXPERTS - 1)
    return pos, row_tok, blk_e, blk_valid


def _rope_tables(positions, rot_dim, pad_value):
    half = rot_dim // 2
    inv = THETA ** (-jnp.arange(0, rot_dim, 2, dtype=F32) / rot_dim)
    ang = positions.astype(F32)[..., None] * inv
    cos, sin = jnp.cos(ang), jnp.sin(ang)
    rest = positions.shape + (LANES - 2 * half,)
    c = jnp.concatenate([cos, cos, jnp.full(rest, pad_value, F32)], -1)
    s = jnp.concatenate([-sin, sin, jnp.zeros(rest, F32)], -1)
    return c, s


def _pad_cols(w, width):
    return jnp.pad(w, ((0, 0), (0, width - w.shape[1])))


def _mixer_weights(w_in, w_uq, w_ukv, w_a, w_b, w_c, tc):
    offs = [0]
    for n in SPLITS:
        offs.append(offs[-1] + n)
    part = lambda j: w_in[:, offs[j]:offs[j + 1]]
    d = w_in.shape[0]
    w1 = _pad_cols(jnp.concatenate([part(0), part(1), part(2)], 1), Q_RANK + KV_RANK + LANES)
    qi = jnp.pad(part(6).reshape(d, H_IDX, D_IDX), ((0, 0), (0, 0), (0, LANES - D_IDX))).reshape(d, H_IDX * LANES)
    w2 = jnp.concatenate([part(3), part(4), part(5), qi, _pad_cols(part(7), LANES), _pad_cols(part(8), LANES)], 1)
    n_cb = CONV_W // tc
    w3 = jnp.stack([part(9).reshape(d, n_cb, tc), part(10).reshape(d, n_cb, tc), part(11).reshape(d, n_cb, tc)],
                   axis=2).reshape(d, 3 * CONV_W)
    wg = part(12).reshape(d, 3, d).transpose(1, 0, 2)
    wuq = jnp.pad(w_uq.reshape(Q_RANK, H_MLA, NOPE + ROPE_MLA),
                  ((0, 0), (0, 0), (0, 2 * LANES - NOPE - ROPE_MLA))).reshape(Q_RANK, H_MLA * 2 * LANES)
    wbr = jnp.stack([w_a, w_b, w_c], 0)
    cast = lambda a: a.astype(CDT)
    return cast(w1), cast(w2), cast(w3), cast(wg), cast(wuq), cast(w_ukv), cast(wbr)


def kernel(x, c, positions, ada_w, ada_b, ln1_g, ln1_b, ln2_g, ln2_b, w_in, mla_q_norm, mla_kv_norm, w_uq, w_ukv,
           conv_w, conv_b, w_branch_a, w_branch_b, w_branch_c, w_o, ffn_w_gate, ffn_w_up, ffn_w_down, router_w,
           moe_w_gate, moe_w_up, moe_w_down):
    bsz, s_len, d = x.shape
    n_tok = bsz * s_len
    depth = ada_w.shape[0]
    conv_tc = 512

    cm, sm = _rope_tables(positions, ROPE_MLA, 0.0)
    cd, sd = _rope_tables(positions, ROT_DSA, 1.0)
    ci, si = _rope_tables(positions, ROT_IDX, 1.0)

    mod = _ada(c, ada_w, ada_b)
    vec = lambda a: a.reshape(1, d)

    h = None
    for i in range(depth):
        sh_m, sc_m, g_m, sh_f, sc_f, g_f = [mod[i, :, j * d:(j + 1) * d].reshape(bsz, 1, d) for j in range(6)]
        if h is None:
            h = _modulate(x, sc_m, sh_m, CDT)
        w1, w2, w3, wg, wuq, wukv, wbr = _mixer_weights(w_in[i], w_uq[i], w_ukv[i], w_branch_a[i], w_branch_b[i],
                                                       w_branch_c[i], conv_tc)

        qa, ka, va = _mla_prep(h, w1, mla_q_norm[i].reshape(1, -1), mla_kv_norm[i].reshape(1, -1), wuq, wukv, cm, sm)
        ya = _mla_attention(qa, ka, va)

        qd, kd, vd, qi, ki, wi = _dsa_prep(h, w2, cd, sd, ci, si)
        yb = _dsa_attention(qi, ki, wi, qd, kd, vd)

        yc = _conv_branch(h, w3, conv_w[i], conv_b[i], conv_tc)

        merged = _merge(h.reshape(n_tok, d), ya.reshape(n_tok, -1), yb.reshape(n_tok, -1), yc.reshape(n_tok, -1),
                        wg, wbr)
        moe_layer = i % 2 == 1
        x, h = _out_ln(merged.reshape(bsz, s_len, d), w_o[i].astype(CDT), x, g_m, vec(ln1_g[i]), vec(ln1_b[i]),
                       sc_f, sh_f, F32 if moe_layer else CDT)

        j = i // 2
        if i + 1 < depth:
            sh_n, sc_n = [mod[i + 1, :, q * d:(q + 1) * d].reshape(bsz, 1, d) for q in range(2)]
        else:
            sh_n, sc_n = jnp.zeros((bsz, 1, d), F32), jnp.zeros((bsz, 1, d), F32)
        if not moe_layer:
            x, h = _ffn_ln(h, ffn_w_gate[j].astype(CDT), ffn_w_up[j].astype(CDT), ffn_w_down[j].astype(CDT), x, g_f,
                           vec(ln2_g[i]), vec(ln2_b[i]), sc_n, sh_n)
        else:
            ht = h.reshape(n_tok, d)
            top_e, top_g = _router(ht, _pad_cols(router_w[j], LANES))
            pos, row_tok, blk_e, blk_valid = _routing_tables(top_e[:, :TOP_K], n_tok)
            y_rows = _moe_ffn(ht, row_tok, blk_e, blk_valid, moe_w_gate[j].astype(CDT), moe_w_up[j].astype(CDT),
                              moe_w_down[j].astype(CDT))
            x = _combine_ln(y_rows, pos, top_g, x, g_f, vec(ln2_g[i]), vec(ln2_b[i]))
            h = None if i + 1 >= depth else _modulate(x, sc_n, sh_n, CDT)
    return x
```

```python
import functools

import jax
import jax.numpy as jnp
import numpy as np
from jax import lax
from jax.experimental import pallas as pl
from jax.experimental.pallas import tpu as pltpu

F32 = jnp.float32
CDT = jnp.bfloat16

D = 2048
DEPTH = 2
H_MLA, NOPE, ROPE_MLA, V_MLA = 8, 128, 64, 128
Q_RANK, KV_RANK = 512, 256
H_DSA, G_DSA, HD_DSA = 8, 2, 128
ROT_DSA = HD_DSA // 4
H_IDX, D_IDX = 8, 64
ROT_IDX = D_IDX // 4
TOPK_MAX = 256
CONV_W, CONV_K = 1024, 3
THETA = 500000.0
N_EXPERTS, TOP_K = 8, 2
ALPHA = (2 * DEPTH) ** 0.25
LN_EPS = 1e-5
RMS_EPS = 1e-6
SPLITS = (Q_RANK, KV_RANK, ROPE_MLA, H_DSA * HD_DSA, G_DSA * HD_DSA, G_DSA * HD_DSA,
          H_IDX * D_IDX, D_IDX, H_IDX, CONV_W, CONV_W, CONV_W, 3 * D)

LANES = 128
NEG = -1e30
LOG2E = 1.4426950408889634
VMEM_LIMIT = 56 * 1024 * 1024

MOE_BLK = 512
MOE_GROUP = 2
MLA_BLK = 512


def _params(sem, vmem=VMEM_LIMIT):
    return pltpu.CompilerParams(dimension_semantics=sem, vmem_limit_bytes=vmem)


def _sigmoid(x):
    return 1.0 / (1.0 + jnp.exp(-x))


def _dot(a, b):
    return jnp.dot(a, b, preferred_element_type=F32)


def _dot_t(a, b):
    return lax.dot_general(a, b, (((1,), (1,)), ((), ())), preferred_element_type=F32)


def _layer_norm(y, g, b):
    mu = jnp.mean(y, -1, keepdims=True)
    d = y - mu
    var = jnp.mean(d * d, -1, keepdims=True)
    return d * lax.rsqrt(var + LN_EPS) * g + b


def _rope(y, c, s, half):
    lane = lax.broadcasted_iota(jnp.int32, y.shape, 1)
    swapped = jnp.where(lane < half, pltpu.roll(y, LANES - half, 1), pltpu.roll(y, half, 1))
    return y * c + swapped * s


def _ada_body(c_ref, w_ref, b_ref, o_ref):
    c = c_ref[...]
    ca = (c * _sigmoid(c)).astype(CDT)
    o_ref[...] = _dot(ca, w_ref[...].astype(CDT)) + b_ref[...]


def _ada(c, ada_w, ada_b):
    depth, d, n = ada_w.shape
    bsz = c.shape[0]
    rows = 8
    cp = jnp.zeros((rows, d), F32).at[:bsz].set(c)
    tn = 1024
    out = pl.pallas_call(
        _ada_body,
        grid=(depth, n // tn),
        in_specs=[pl.BlockSpec((rows, d), lambda l, j: (0, 0)),
                  pl.BlockSpec((None, d, tn), lambda l, j: (l, 0, j)),
                  pl.BlockSpec((None, 1, tn), lambda l, j: (l, 0, j))],
        out_specs=pl.BlockSpec((None, rows, tn), lambda l, j: (l, 0, j)),
        out_shape=jax.ShapeDtypeStruct((depth, rows, n), F32),
        compiler_params=_params(("arbitrary", "arbitrary")),
        name="ada_mod",
    )(cp, ada_w, ada_b.reshape(depth, 1, n))
    return out[:, :bsz]


def _mod_body(x_ref, sc_ref, sh_ref, o_ref):
    o_ref[...] = (x_ref[...] * (1.0 + sc_ref[...]) + sh_ref[...]).astype(o_ref.dtype)


def _modulate(x, sc, sh, out_dtype):
    bsz, s_len, d = x.shape
    tm = 512
    vec = pl.BlockSpec((None, 1, d), lambda b, i: (b, 0, 0))
    return pl.pallas_call(
        _mod_body,
        grid=(bsz, s_len // tm),
        in_specs=[pl.BlockSpec((None, tm, d), lambda b, i: (b, i, 0)), vec, vec],
        out_specs=pl.BlockSpec((None, tm, d), lambda b, i: (b, i, 0)),
        out_shape=jax.ShapeDtypeStruct(x.shape, out_dtype),
        compiler_params=_params(("arbitrary", "arbitrary")),
        name="modulate",
    )(x, sc, sh)


def _mla_prep_body(h_ref, w1_ref, qg_ref, kg_ref, wuq_ref, wukv_ref, c_ref, s_ref, q_ref, k_ref, v_ref):
    z = _dot(h_ref[...], w1_ref[...])
    cq = z[:, :Q_RANK]
    ckv = z[:, Q_RANK:Q_RANK + KV_RANK]
    kr = z[:, Q_RANK + KV_RANK:]
    nq = (cq * lax.rsqrt(jnp.mean(cq * cq, -1, keepdims=True) + RMS_EPS) * qg_ref[...]).astype(CDT)
    nkv = (ckv * lax.rsqrt(jnp.mean(ckv * ckv, -1, keepdims=True) + RMS_EPS) * kg_ref[...]).astype(CDT)
    qa = _dot(nq, wuq_ref[...])
    kva = _dot(nkv, wukv_ref[...])
    c = c_ref[...]
    s = s_ref[...]
    scale = (NOPE + ROPE_MLA) ** -0.5 * LOG2E
    krot_t = _rope(kr, c, s, ROPE_MLA // 2).T.astype(CDT)
    lane = lax.broadcasted_iota(jnp.int32, (h_ref.shape[0], LANES), 1)
    ones_col = jnp.where(lane == 0, 1.0, 0.0).astype(CDT)
    for h in range(H_MLA):
        lo = h * 2 * LANES
        q_ref[h, :, 0:LANES] = (qa[:, lo:lo + LANES] * scale).astype(CDT)
        q_ref[h, :, LANES:2 * LANES] = (_rope(qa[:, lo + LANES:lo + 2 * LANES], c, s, ROPE_MLA // 2)
                                        * scale).astype(CDT)
        k_ref[h, 0:LANES, :] = kva[:, lo:lo + LANES].T.astype(CDT)
        k_ref[h, LANES:2 * LANES, :] = krot_t
        v_ref[h, :, 0:LANES] = kva[:, lo + LANES:lo + 2 * LANES].astype(CDT)
        v_ref[h, :, LANES:2 * LANES] = ones_col


def _mla_prep(h, w1, qg, kg, wuq, wukv, ctab, stab):
    bsz, s_len, d = h.shape
    tm = min(MLA_BLK, s_len)
    full = lambda a: pl.BlockSpec(a.shape, lambda b, i: (0,) * a.ndim)
    row = lambda w: pl.BlockSpec((None, tm, w), lambda b, i: (b, i, 0))
    head = lambda w: pl.BlockSpec((None, H_MLA, tm, w), lambda b, i: (b, 0, i, 0))
    return pl.pallas_call(
        _mla_prep_body,
        grid=(bsz, s_len // tm),
        in_specs=[row(d), full(w1), full(qg), full(kg), full(wuq), full(wukv), row(LANES), row(LANES)],
        out_specs=[head(2 * LANES),
                   pl.BlockSpec((None, H_MLA, None, 2 * LANES, tm), lambda b, i: (b, 0, i, 0, 0)),
                   head(2 * LANES)],
        out_shape=[jax.ShapeDtypeStruct((bsz, H_MLA, s_len, 2 * LANES), CDT),
                   jax.ShapeDtypeStruct((bsz, H_MLA, s_len // tm, 2 * LANES, tm), CDT),
                   jax.ShapeDtypeStruct((bsz, H_MLA, s_len, 2 * LANES), CDT)],
        compiler_params=_params(("arbitrary", "arbitrary")),
        name="mla_prep",
    )(h, w1, qg, kg, wuq, wukv, ctab, stab)


def _flash_body(q_ref, k_ref, v_ref, o_ref, s_ref, mx_ref, m_ref, acc_ref, *, blk):
    qi = pl.program_id(2)
    dv = V_MLA

    def scores(hd, j, masked):
        s = _dot(q_ref[hd], k_ref[hd, j])
        if masked:
            row = lax.broadcasted_iota(jnp.int32, s.shape, 0)
            col = lax.broadcasted_iota(jnp.int32, s.shape, 1)
            s = jnp.where(col <= row, s, NEG)
        return s, jnp.max(s, -1, keepdims=True)

    def put(hd, sv):
        s_ref[hd], mx_ref[hd] = sv

    def accumulate(hd, j):
        off = pl.multiple_of(j * blk, blk)
        m_old = m_ref[hd]
        m_new = jnp.maximum(m_old, mx_ref[hd])
        alpha = jnp.exp2(m_old - m_new)
        p = jnp.exp2((s_ref[hd] - m_new).astype(CDT))
        acc_ref[hd] = alpha * acc_ref[hd] + _dot(p, v_ref[hd, pl.ds(off, blk), :])
        m_ref[hd] = m_new

    m_ref[...] = jnp.full(m_ref.shape, NEG, F32)
    acc_ref[...] = jnp.zeros(acc_ref.shape, F32)
    put(0, scores(0, qi, True))
    put(1, scores(1, qi, True))

    def body(n, _):
        j_prev = jnp.where(n == 1, qi, n - 2)
        accumulate(0, j_prev)
        accumulate(1, j_prev)
        put(0, scores(0, n - 1, False))
        put(1, scores(1, n - 1, False))
        return 0

    lax.fori_loop(1, qi + 1, body, 0)
    j_last = jnp.where(qi == 0, qi, qi - 1)
    accumulate(0, j_last)
    accumulate(1, j_last)
    for hd in range(2):
        acc = acc_ref[hd]
        o_ref[:, hd * dv:(hd + 1) * dv] = (acc[:, :dv] / acc[:, dv:dv + 1]).astype(o_ref.dtype)


def _mla_attention(q, k, v):
    bsz, nh, s_len, dq = q.shape
    dve = v.shape[-1]
    dv = V_MLA
    blk = min(MLA_BLK, s_len)
    return pl.pallas_call(
        functools.partial(_flash_body, blk=blk),
        grid=(bsz, nh // 2, s_len // blk),
        in_specs=[pl.BlockSpec((None, 2, blk, dq), lambda b, h, i: (b, h, i, 0)),
                  pl.BlockSpec((None, 2, s_len // blk, dq, blk), lambda b, h, i: (b, h, 0, 0, 0)),
                  pl.BlockSpec((None, 2, s_len, dve), lambda b, h, i: (b, h, 0, 0))],
        out_specs=pl.BlockSpec((None, blk, 2 * dv), lambda b, h, i: (b, i, h)),
        out_shape=jax.ShapeDtypeStruct((bsz, s_len, nh * dv), CDT),
        scratch_shapes=[pltpu.VMEM((2, blk, blk), F32), pltpu.VMEM((2, blk, 1), F32),
                        pltpu.VMEM((2, blk, 1), F32), pltpu.VMEM((2, blk, dve), F32)],
        compiler_params=_params(("arbitrary", "arbitrary", "arbitrary")),
        name="mla_attention",
    )(q, k, v)


DSA_CHUNK = 512
VT_PAD = 16
VT_ROWS = HD_DSA + VT_PAD


def _dsa_prep_body(h_ref, w2_ref, cd_ref, sd_ref, ci_ref, si_ref, q_ref, k_ref, vt_ref, qi_ref, ki_ref, wit_ref):
    z = _dot(h_ref[...], w2_ref[...])
    cd, sd, ci, si = cd_ref[...], sd_ref[...], ci_ref[...], si_ref[...]
    col = lambda j: z[:, j * LANES:(j + 1) * LANES]
    for h in range(H_DSA):
        q_ref[h] = (_rope(col(h), cd, sd, ROT_DSA // 2) * (HD_DSA ** -0.5 * LOG2E)).T.astype(CDT)
    for g in range(G_DSA):
        k_ref[:, g * LANES:(g + 1) * LANES] = _rope(col(H_DSA + g), cd, sd, ROT_DSA // 2).astype(CDT)
    base = H_DSA + G_DSA
    sub = lax.broadcasted_iota(jnp.int32, (VT_PAD, z.shape[0]), 0)
    ones_row = jnp.where(sub == 0, 1.0, 0.0).astype(CDT)
    for g in range(G_DSA):
        vt_ref[g * VT_ROWS:g * VT_ROWS + LANES, :] = col(base + g).T.astype(CDT)
        vt_ref[g * VT_ROWS + LANES:(g + 1) * VT_ROWS, :] = ones_row
    base += G_DSA
    for h in range(H_IDX):
        qi_ref[h] = (_rope(col(base + h), ci, si, ROT_IDX // 2) * D_IDX ** -0.5).T.astype(CDT)
    base += H_IDX
    ki_ref[...] = _rope(col(base), ci, si, ROT_IDX // 2).astype(CDT)
    wit_ref[...] = col(base + 1).T[0:H_IDX, :] * H_IDX ** -0.5


def _dsa_prep(h, w2, cd, sd, ci, si):
    bsz, s_len, d = h.shape
    tm = min(DSA_CHUNK, s_len)
    row = lambda w: pl.BlockSpec((None, tm, w), lambda b, i: (b, i, 0))
    return pl.pallas_call(
        _dsa_prep_body,
        grid=(bsz, s_len // tm),
        in_specs=[row(d), pl.BlockSpec(w2.shape, lambda b, i: (0, 0)), row(LANES), row(LANES), row(LANES),
                  row(LANES)],
        out_specs=[pl.BlockSpec((None, H_DSA, LANES, tm), lambda b, i: (b, 0, 0, i)), row(G_DSA * LANES),
                   pl.BlockSpec((None, None, G_DSA * VT_ROWS, tm), lambda b, i: (b, i, 0, 0)),
                   pl.BlockSpec((None, H_IDX, LANES, tm), lambda b, i: (b, 0, 0, i)), row(LANES),
                   pl.BlockSpec((None, H_IDX, tm), lambda b, i: (b, 0, i))],
        out_shape=[jax.ShapeDtypeStruct((bsz, H_DSA, LANES, s_len), CDT),
                   jax.ShapeDtypeStruct((bsz, s_len, G_DSA * LANES), CDT),
                   jax.ShapeDtypeStruct((bsz, s_len // tm, G_DSA * VT_ROWS, tm), CDT),
                   jax.ShapeDtypeStruct((bsz, H_IDX, LANES, s_len), CDT),
                   jax.ShapeDtypeStruct((bsz, s_len, LANES), CDT),
                   jax.ShapeDtypeStruct((bsz, H_IDX, s_len), F32)],
        compiler_params=_params(("arbitrary", "arbitrary")),
        name="dsa_prep",
    )(h, w2, cd, sd, ci, si)


FOLD_ROWS = 64


def _fold(x):
    return jnp.sum(x.reshape(x.shape[0] // FOLD_ROWS, FOLD_ROWS, x.shape[1]), axis=0)


def _dsa_body(qi_ref, ki_ref, wit_ref, q_ref, k_ref, vt_ref, tri_ref, o_ref,
              key_ref, sel_ref, qia_ref, qg_ref, s_ref, mx_ref, m_ref, acc_ref, *, tq, tkc, n_sel):
    i = pl.program_id(1)
    q0 = i * tq
    nkc = (q0 + tq + tkc - 1) // tkc
    n_rep = H_DSA // G_DSA
    key_neg_inf = jnp.int32(-8388608) ^ jnp.int32(0x7FFFFFFF)
    qpos = q0 + lax.broadcasted_iota(jnp.int32, (tkc, tq), 1)

    for h in range(H_IDX):
        qia_ref[:, h * tq:(h + 1) * tq] = qi_ref[h]
    for g in range(G_DSA):
        for r in range(n_rep):
            qg_ref[g, :, r * tq:(r + 1) * tq] = q_ref[g * n_rep + r]

    def score_chunk(c, _):
        off = pl.multiple_of(c * tkc, tkc)
        kc = ki_ref[pl.ds(off, tkc), :]
        w = wit_ref[...]
        lg = _dot(kc, qia_ref[...])
        sc = jnp.zeros((tkc, tq), F32)
        for h in range(H_IDX):
            sc = sc + w[h:h + 1, :] * jnp.maximum(lg[:, h * tq:(h + 1) * tq], 0.0)
        kpos = off + lax.broadcasted_iota(jnp.int32, (tkc, tq), 0)
        sc = jnp.where(kpos <= qpos, sc, -jnp.inf)
        bits = pltpu.bitcast(sc, jnp.int32)
        key_ref[c] = jnp.where(bits >= 0, bits, bits ^ jnp.int32(0x7FFFFFFF))
        return 0

    lax.fori_loop(0, nkc, score_chunk, 0)

    def count(pred):
        def body(c, cnt):
            return cnt + _fold(jnp.where(pred(key_ref[c]), 1.0, 0.0))
        part = lax.fori_loop(0, nkc, body, jnp.zeros((FOLD_ROWS, tq), F32))
        return jnp.sum(part, axis=0, keepdims=True)

    def bit_step(bi, thr):
        cand = thr + lax.shift_left(jnp.int32(1), 31 - bi)
        cnt = count(lambda kk: kk >= cand)
        return jnp.where(cnt >= n_sel, cand, thr)

    thr = lax.fori_loop(0, 32, bit_step, jnp.full((1, tq), jnp.iinfo(jnp.int32).min, jnp.int32))

    cnt_gt = count(lambda kk: kk > thr)
    cnt_eq = count(lambda kk: kk == thr)
    need = n_sel - cnt_gt
    excess = jnp.where(thr > key_neg_inf, cnt_eq - need, 0.0)

    def mask_plain():
        def body(c, _):
            kk = key_ref[c]
            sel_ref[c] = jnp.where(kk >= thr, jnp.where(kk > key_neg_inf, 0.0, NEG), NEG)
            return 0
        lax.fori_loop(0, nkc, body, 0)

    def mask_ranked():
        def body(c, seen):
            kk = key_ref[c]
            tie = kk == thr
            pre = _dot(tri_ref[...], jnp.where(tie, 1.0, 0.0).astype(CDT))
            rank = seen + pre
            keep_tie = jnp.where(tie, jnp.where(rank <= need, 0.0, NEG), NEG)
            keep = jnp.where(kk > thr, 0.0, keep_tie)
            sel_ref[c] = jnp.where(kk > key_neg_inf, keep, NEG)
            return seen + pre[tkc - 1:tkc, :]
        lax.fori_loop(0, nkc, body, jnp.zeros((1, tq), F32))

    lax.cond(jnp.max(excess) > 0.0, mask_ranked, mask_plain)

    m_ref[...] = jnp.full(m_ref.shape, NEG, F32)
    acc_ref[...] = jnp.zeros(acc_ref.shape, F32)

    def scores(g, c):
        off = pl.multiple_of(c * tkc, tkc)
        msk = sel_ref[c]
        msk = jnp.concatenate([msk] * n_rep, axis=1)
        kg = k_ref[pl.ds(off, tkc), g * LANES:(g + 1) * LANES]
        s = _dot(kg, qg_ref[g]) + msk
        return s, jnp.max(s, axis=0, keepdims=True)

    def put(g, sv):
        s_ref[g], mx_ref[g] = sv

    def accumulate(g, c):
        m_old = m_ref[g]
        m_new = jnp.maximum(m_old, mx_ref[g])
        alpha = jnp.exp2(m_old - m_new)
        p = jnp.exp2((s_ref[g] - m_new).astype(CDT))
        acc_ref[g] = alpha * acc_ref[g] + _dot(vt_ref[c, g * VT_ROWS:(g + 1) * VT_ROWS, :], p)
        m_ref[g] = m_new

    put(0, scores(0, 0))
    put(1, scores(1, 0))

    def att_chunk(c, _):
        sv0 = scores(0, c)
        sv1 = scores(1, c)
        accumulate(0, c - 1)
        accumulate(1, c - 1)
        put(0, sv0)
        put(1, sv1)
        return 0

    lax.fori_loop(1, nkc, att_chunk, 0)
    accumulate(0, nkc - 1)
    accumulate(1, nkc - 1)
    for g in range(G_DSA):
        acc = acc_ref[g]
        out = acc[0:HD_DSA, :] / acc[HD_DSA:HD_DSA + 1, :]
        for r in range(n_rep):
            hh = g * n_rep + r
            o_ref[:, hh * LANES:(hh + 1) * LANES] = out[:, r * tq:(r + 1) * tq].T.astype(o_ref.dtype)


def _dsa_attention(qi, ki, wit, q, k, vt):
    bsz, s_len, _ = k.shape
    tq = 128
    tkc = min(DSA_CHUNK, s_len)
    n_chunks = s_len // tkc
    n_sel = min(TOPK_MAX, s_len // 4)
    n_rep = H_DSA // G_DSA
    tri =(lax.broadcasted_iota(jnp.int32, (tkc, tkc), 1)
           <= lax.broadcasted_iota(jnp.int32, (tkc, tkc), 0)).astype(CDT)
    row = lambda w: pl.BlockSpec((None, tq, w), lambda b, i: (b, i, 0))
    seq = lambda w: pl.BlockSpec((None, s_len, w), lambda b, i: (b, 0, 0))
    return pl.pallas_call(
        functools.partial(_dsa_body, tq=tq, tkc=tkc, n_sel=n_sel),
        grid=(bsz, s_len // tq),
        in_specs=[pl.BlockSpec((None, H_IDX, LANES, tq), lambda b, i: (b, 0, 0, i)), seq(LANES),
                  pl.BlockSpec((None, H_IDX, tq), lambda b, i: (b, 0, i)),
                  pl.BlockSpec((None, H_DSA, LANES, tq), lambda b, i: (b, 0, 0, i)), seq(G_DSA * LANES),
                  pl.BlockSpec((None, n_chunks, G_DSA * VT_ROWS, tkc), lambda b, i: (b, 0, 0, 0)),
                  pl.BlockSpec((tkc, tkc), lambda b, i: (0, 0))],
        out_specs=row(H_DSA * LANES),
        out_shape=jax.ShapeDtypeStruct((bsz, s_len, H_DSA * LANES), CDT),
        scratch_shapes=[pltpu.VMEM((n_chunks, tkc, tq), jnp.int32),
                        pltpu.VMEM((n_chunks, tkc, tq), F32),
                        pltpu.VMEM((LANES, H_IDX * tq), CDT),
                        pltpu.VMEM((G_DSA, LANES, n_rep * tq), CDT),
                        pltpu.VMEM((G_DSA, tkc, n_rep * tq), F32),
                        pltpu.VMEM((G_DSA, 1, n_rep * tq), F32),
                        pltpu.VMEM((G_DSA, 1, n_rep * tq), F32),
                        pltpu.VMEM((G_DSA, VT_ROWS, n_rep * tq), F32)],
        compiler_params=_params(("arbitrary", "arbitrary")),
        name="dsa_attention",
    )(qi, ki, wit, q, k, vt, tri)


def _conv_body(h_ref, w_ref, cw_ref, cb_ref, o_ref, tail_ref, *, tm, tc):
    @pl.when(pl.program_id(2) == 0)
    def _():
        tail_ref[...] = jnp.zeros(tail_ref.shape, F32)

    z = _dot(h_ref[...], w_ref[...])
    gb = z[:, :tc]
    u = z[:, tc:2 * tc] * z[:, 2 * tc:]
    prev = tail_ref[...]
    row = lax.broadcasted_iota(jnp.int32, (tm, tc), 0)
    u1 = jnp.where(row == 0, prev[7:8, :], pltpu.roll(u, 1, 0))
    u2 = jnp.where(row == 0, prev[6:7, :], jnp.where(row == 1, prev[7:8, :], pltpu.roll(u, 2, 0)))
    cw = cw_ref[...]
    y = cw[0:1, :] * u2 + cw[1:2, :] * u1 + cw[2:3, :] * u + cb_ref[...]
    o_ref[...] = (gb * y).astype(o_ref.dtype)
    tail_ref[...] = u[tm - 8:tm, :]


def _conv_branch(h, w3, conv_w, conv_b, tc):
    bsz, s_len, d = h.shape
    tm = 512
    n_cb = CONV_W // tc
    return pl.pallas_call(
        functools.partial(_conv_body, tm=tm, tc=tc),
        grid=(n_cb, bsz, s_len // tm),
        in_specs=[pl.BlockSpec((None, tm, d), lambda cb, b, i: (b, i, 0)),
                  pl.BlockSpec((d, 3 * tc), lambda cb, b, i: (0, cb)),
                  pl.BlockSpec((CONV_K, tc), lambda cb, b, i: (0, cb)),
                  pl.BlockSpec((1, tc), lambda cb, b, i: (0, cb))],
        out_specs=pl.BlockSpec((None, tm, tc), lambda cb, b, i: (b, i, cb)),
        out_shape=jax.ShapeDtypeStruct((bsz, s_len, CONV_W), CDT),
        scratch_shapes=[pltpu.VMEM((8, tc), F32)],
        compiler_params=_params(("arbitrary", "arbitrary", "arbitrary")),
        name="conv_branch",
    )(h, w3, conv_w, conv_b.reshape(1, CONV_W))


def _merge_body(h_ref, ya_ref, yb_ref, yc_ref, wg_ref, wbr_ref, o_ref):
    h = h_ref[...]
    acc = None
    for j, y_ref in enumerate((ya_ref, yb_ref, yc_ref)):
        term = _sigmoid(_dot(h, wg_ref[j])) * _dot(y_ref[...], wbr_ref[j])
        acc = term if acc is None else acc + term
    o_ref[...] = acc.astype(o_ref.dtype)


def _merge(h, ya, yb, yc, wg, wbr):
    t, d = h.shape
    tm, tn = 512, 512
    yw = ya.shape[1]
    row = lambda w: pl.BlockSpec((tm, w), lambda j, i: (i, 0))
    return pl.pallas_call(
        _merge_body,
        grid=(d // tn, t // tm),
        in_specs=[row(d), row(yw), row(yw), row(yw),
                  pl.BlockSpec((3, d, tn), lambda j, i: (0, 0, j)),
                  pl.BlockSpec((3, yw, tn), lambda j, i: (0, 0, j))],
        out_specs=pl.BlockSpec((tm, tn), lambda j, i: (i, j)),
        out_shape=jax.ShapeDtypeStruct((t, d), CDT),
        compiler_params=_params(("arbitrary", "arbitrary")),
        name="gated_merge",
    )(h, ya, yb, yc, wg, wbr)


def _out_ln_body(m_ref, wo_ref, x_ref, gm_ref, lg_ref, lb_ref, sc_ref, sh_ref, x1_ref, h_ref):
    y = ALPHA * x_ref[...] + gm_ref[...] * _dot(m_ref[...], wo_ref[...])
    x1 = _layer_norm(y, lg_ref[...], lb_ref[...])
    x1_ref[...] = x1
    h_ref[...] = (x1 * (1.0 + sc_ref[...]) + sh_ref[...]).astype(h_ref.dtype)


def _out_ln(merged, wo, x, gm, lg, lb, sc, sh, h_dtype):
    bsz, s_len, d = x.shape
    tm = 256
    row = pl.BlockSpec((None, tm, d), lambda b, i: (b, i, 0))
    per_b = pl.BlockSpec((None, 1, d), lambda b, i: (b, 0, 0))
    vec = pl.BlockSpec((1, d), lambda b, i: (0, 0))
    return pl.pallas_call(
        _out_ln_body,
        grid=(bsz, s_len // tm),
        in_specs=[row, pl.BlockSpec((d, d), lambda b, i: (0, 0)), row, per_b, vec, vec, per_b, per_b],
        out_specs=[row, row],
        out_shape=[jax.ShapeDtypeStruct(x.shape, F32), jax.ShapeDtypeStruct(x.shape, h_dtype)],
        compiler_params=_params(("arbitrary", "arbitrary")),
        name="out_proj_ln",
    )(merged, wo, x, gm, lg, lb, sc, sh)


def _ffn_body(h_ref, wg_ref, wu_ref, wd_ref, x_ref, gf_ref, lg_ref, lb_ref, sc_ref, sh_ref, x2_ref, h2_ref,
              acc_ref):
    k = pl.program_id(2)

    @pl.when(k == 0)
    def _():
        acc_ref[...] = jnp.zeros(acc_ref.shape, F32)

    h = h_ref[...]
    a = _dot(h, wg_ref[...])
    act = (a * _sigmoid(a) * _dot(h, wu_ref[...])).astype(CDT)
    acc_ref[...] += _dot(act, wd_ref[...])

    @pl.when(k == pl.num_programs(2) - 1)
    def _():
        y = ALPHA * x_ref[...] + gf_ref[...] * acc_ref[...]
        x2 = _layer_norm(y, lg_ref[...], lb_ref[...])
        x2_ref[...] = x2
        h2_ref[...] = (x2 * (1.0 + sc_ref[...]) + sh_ref[...]).astype(h2_ref.dtype)


def _ffn_ln(h, wg, wu, wd, x, gf, lg, lb, sc, sh):
    bsz, s_len, d = x.shape
    f = wg.shape[1]
    tm, tf = 512, 512
    row = pl.BlockSpec((None, tm, d), lambda b, i, k: (b, i, 0))
    per_b = pl.BlockSpec((None, 1, d), lambda b, i, k: (b, 0, 0))
    vec = pl.BlockSpec((1, d), lambda b, i, k: (0, 0))
    return pl.pallas_call(
        _ffn_body,
        grid=(bsz, s_len // tm, f // tf),
        in_specs=[row, pl.BlockSpec((d, tf), lambda b, i, k: (0, k)), pl.BlockSpec((d, tf), lambda b, i, k: (0, k)),
                  pl.BlockSpec((tf, d), lambda b, i, k: (k, 0)), row, per_b, vec, vec, per_b, per_b],
        out_specs=[row, row],
        out_shape=[jax.ShapeDtypeStruct(x.shape, F32), jax.ShapeDtypeStruct(x.shape, CDT)],
        scratch_shapes=[pltpu.VMEM((tm, d), F32)],
        compiler_params=_params(("arbitrary", "arbitrary", "arbitrary")),
        name="ffn_ln",
    )(h, wg, wu, wd, x, gf, lg, lb, sc, sh)


def _router_body(h_ref, wr_ref, e_ref, g_ref):
    lg = jnp.dot(h_ref[...], wr_ref[...], preferred_element_type=F32, precision=lax.Precision.HIGHEST)
    lane = lax.broadcasted_iota(jnp.int32, lg.shape, 1)
    lanef = lane.astype(F32)
    lg = jnp.where(lane < N_EXPERTS, lg, -jnp.inf)
    m1 = jnp.max(lg, -1, keepdims=True)
    i1 = jnp.min(jnp.where(lg == m1, lanef, float(LANES)), -1, keepdims=True)
    lg2 = jnp.where(lanef == i1, -jnp.inf, lg)
    m2 = jnp.max(lg2, -1, keepdims=True)
    i2 = jnp.min(jnp.where(lg2 == m2, lanef, float(LANES)), -1, keepdims=True)
    e = jnp.exp(m2 - m1)
    g1 = 1.0 / (1.0 + e)
    g2 = e / (1.0 + e)
    e_ref[...] = jnp.where(lane == 0, i1, jnp.where(lane == 1, i2, 0.0)).astype(jnp.int32)
    g_ref[...] = jnp.where(lane == 0, g1, jnp.where(lane == 1, g2, 0.0))


def _router(h, wr):
    t, d = h.shape
    tm = 512
    return pl.pallas_call(
        _router_body,
        grid=(t // tm,),
        in_specs=[pl.BlockSpec((tm, d), lambda i: (i, 0)), pl.BlockSpec((d, LANES), lambda i: (0, 0))],
        out_specs=[pl.BlockSpec((tm, LANES), lambda i: (i, 0)), pl.BlockSpec((tm, LANES), lambda i: (i, 0))],
        out_shape=[jax.ShapeDtypeStruct((t, LANES), jnp.int32), jax.ShapeDtypeStruct((t, LANES), F32)],
        compiler_params=_params(("arbitrary",)),
        name="moe_router",
    )(h, wr)


def _moe_body(tok_ref, ge_ref, bv_ref, h_hbm, wg_ref, wu_ref, wd_ref, y_ref, xbuf, xs, sem, *, blk):
    i = pl.program_id(0)
    k = pl.program_id(1)
    ng = pl.num_programs(0)
    rows = MOE_GROUP * blk

    def issue(gi):
        def body(r, _):
            pltpu.make_async_copy(h_hbm.at[pl.ds(tok_ref[gi * rows + r], 1), :], xbuf.at[pl.ds(r, 1), :], sem).start()
            return 0
        lax.fori_loop(0, rows, body, 0)

    @pl.when(k == 0)
    def _():
        @pl.when(i == 0)
        def _():
            issue(0)

        pltpu.make_async_copy(h_hbm.at[pl.ds(0, rows), :], xbuf, sem).wait()
        xs[...] = xbuf[...].astype(CDT)

        @pl.when(i + 1 < ng)
        def _():
            issue(i + 1)

        y_ref[...] = jnp.zeros(y_ref.shape, F32)

    def ffn(n_blocks):
        wg = wg_ref[...].astype(CDT)
        wu = wu_ref[...].astype(CDT)
        wd = wd_ref[...].astype(CDT)
        for b in range(n_blocks):
            x = xs[b * blk:(b + 1) * blk, :]
            a = _dot(x, wg)
            act = (a * _sigmoid(a) * _dot(x, wu)).astype(CDT)
            y_ref[b * blk:(b + 1) * blk, :] += _dot(act, wd)

    live = bv_ref[i * MOE_GROUP]
    for b in range(1, MOE_GROUP):
        live = live + bv_ref[i * MOE_GROUP + b]
    for n_blocks in range(1, MOE_GROUP + 1):
        pl.when(live == n_blocks)(functools.partial(ffn, n_blocks))


def _moe_ffn(h, row_tok, grp_e, blk_valid, wg, wu, wd):
    t, d = h.shape
    n_rows = row_tok.shape[0]
    blk = MOE_BLK
    rows = MOE_GROUP * blk
    ng = n_rows // rows
    f = wg.shape[2]
    tf = 256
    nk = f // tf
    kk = lambda k, i, bv: jnp.where(bv[i * MOE_GROUP] == 1, k, nk - 1)
    grid_spec = pltpu.PrefetchScalarGridSpec(
        num_scalar_prefetch=3,
        grid=(ng, nk),
        in_specs=[pl.BlockSpec(memory_space=pl.ANY),
                  pl.BlockSpec((None, d, tf), lambda i, k, tok, ge, bv: (ge[i], 0, kk(k, i, bv))),
                  pl.BlockSpec((None, d, tf), lambda i, k, tok, ge, bv: (ge[i], 0, kk(k, i, bv))),
                  pl.BlockSpec((None, tf, d), lambda i, k, tok, ge, bv: (ge[i], kk(k, i, bv), 0))],
        out_specs=pl.BlockSpec((rows, d), lambda i, k, tok, ge, bv: (i, 0)),
        scratch_shapes=[pltpu.VMEM((rows, d), F32), pltpu.VMEM((rows, d), CDT), pltpu.SemaphoreType.DMA(())],
    )
    return pl.pallas_call(
        functools.partial(_moe_body, blk=blk),
        grid_spec=grid_spec,
        out_shape=jax.ShapeDtypeStruct((n_rows, d), F32),
        compiler_params=_params(("arbitrary", "arbitrary")),
        name="moe_grouped_ffn",
    )(row_tok, grp_e, blk_valid, h, wg, wu, wd)


def _combine_body(pos_ref, y_hbm, gate_ref, x_ref, gf_ref, lg_ref, lb_ref, o_ref, ybuf, sem, *, tm):
    i = pl.program_id(0)
    nb = pl.num_programs(0)

    def issue(bi, slot):
        def body(r, _):
            for s in range(TOP_K):
                p = pos_ref[(bi * tm + r) * TOP_K + s]
                pltpu.make_async_copy(y_hbm.at[pl.ds(p, 1), :], ybuf.at[slot, s, pl.ds(r, 1), :],
                                      sem.at[slot]).start()
            return 0
        lax.fori_loop(0, tm, body, 0)

    @pl.when(i == 0)
    def _():
        issue(0, 0)

    slot = i % 2
    for s in range(TOP_K):
        pltpu.make_async_copy(y_hbm.at[pl.ds(0, tm), :], ybuf.at[slot, s], sem.at[slot]).wait()

    @pl.when(i + 1 < nb)
    def _():
        issue(i + 1, (i + 1) % 2)

    gate = gate_ref[...]
    f = ybuf[slot, 0] * gate[:, 0:1] + ybuf[slot, 1] * gate[:, 1:2]
    y = ALPHA * x_ref[...] + gf_ref[...] * f
    o_ref[...] = _layer_norm(y, lg_ref[...], lb_ref[...])


def _combine_ln(y_rows, pos, gate, x, gf, lg, lb):
    bsz, s_len, d = x.shape
    tm = 256
    nsb = s_len // tm
    grid_spec = pltpu.PrefetchScalarGridSpec(
        num_scalar_prefetch=1,
        grid=(bsz * nsb,),
        in_specs=[pl.BlockSpec(memory_space=pl.ANY),
                  pl.BlockSpec((tm, LANES), lambda i, pos: (i, 0)),
                  pl.BlockSpec((None, tm, d), lambda i, pos: (i // nsb, i % nsb, 0)),
                  pl.BlockSpec((None, 1, d), lambda i, pos: (i // nsb, 0, 0)),
                  pl.BlockSpec((1, d), lambda i, pos: (0, 0)),
                  pl.BlockSpec((1, d), lambda i, pos: (0, 0))],
        out_specs=pl.BlockSpec((None, tm, d), lambda i, pos: (i // nsb, i % nsb, 0)),
        scratch_shapes=[pltpu.VMEM((2, TOP_K, tm, d), F32), pltpu.SemaphoreType.DMA((2,))],
    )
    return pl.pallas_call(
        functools.partial(_combine_body, tm=tm),
        grid_spec=grid_spec,
        out_shape=jax.ShapeDtypeStruct(x.shape, F32),
        compiler_params=_params(("arbitrary",)),
        name="moe_combine_ln",
    )(pos, y_rows, gate, x, gf, lg, lb)


def _routing_tables(top_e, n_tok):
    blk = MOE_BLK
    rows = MOE_GROUP * blk
    flat_e = top_e.reshape(-1)
    onehot = (flat_e[:, None] == jnp.arange(N_EXPERTS, dtype=jnp.int32)[None, :]).astype(jnp.int32)
    csum = jnp.cumsum(onehot, axis=0)
    counts = csum[-1]
    rank = jnp.sum(csum * onehot, axis=1) - 1
    padded = (counts + rows - 1) // rows * rows
    pend = jnp.cumsum(padded)
    pstart = pend - padded
    pos = (jnp.sum(pstart[None, :] * onehot, axis=1) + rank).astype(jnp.int32)
    n_rows = n_tok * TOP_K + N_EXPERTS * rows
    flat_tok = jnp.repeat(jnp.arange(n_tok, dtype=jnp.int32), TOP_K)
    row_tok = jnp.zeros((n_rows,), jnp.int32).at[pos].set(flat_tok)
    grp_start = jnp.arange(n_rows // rows, dtype=jnp.int32) * rows
    last_start = jnp.maximum(pend[-1] - rows, 0)
    grp_e = jnp.sum((pend[None, :] <= jnp.minimum(grp_start, last_start)[:, None]).astype(jnp.int32), axis=1)
    grp_e = jnp.minimum(grp_e, N_EXPERTS - 1)
    blk_start = jnp.arange(n_rows // blk, dtype=jnp.int32) * blk
    blk_e = jnp.repeat(grp_e, MOE_GROUP)
    blk_valid = ((blk_start < pend[-1]) & (blk_start - pstart[blk_e] < counts[blk_e])).astype(jnp.int32)
    return pos, row_tok, grp_e, blk_valid


def _rope_tables(positions, rot_dim, pad_value):
    half = rot_dim // 2
    inv = jnp.asarray(THETA ** (-np.arange(0, rot_dim, 2, dtype=np.float32) / np.float32(rot_dim)), F32)
    ang = positions.astype(F32)[..., None] * inv
    cos, sin = jnp.cos(ang), jnp.sin(ang)
    rest = positions.shape + (LANES - 2 * half,)
    c = jnp.concatenate([cos, cos, jnp.full(rest, pad_value, F32)], -1)
    s = jnp.concatenate([-sin, sin, jnp.zeros(rest, F32)], -1)
    return c, s


def _pad_cols(w, width):
    return jnp.pad(w, ((0, 0), (0, width - w.shape[1])))


def _mixer_weights(w_in, w_uq, w_ukv, w_a, w_b, w_c, tc):
    offs = [0]
    for n in SPLITS:
        offs.append(offs[-1] + n)
    part = lambda j: w_in[:, offs[j]:offs[j + 1]]
    d = w_in.shape[0]
    w1 = _pad_cols(jnp.concatenate([part(0), part(1), part(2)], 1), Q_RANK + KV_RANK + LANES)
    qi = jnp.pad(part(6).reshape(d, H_IDX, D_IDX), ((0, 0), (0, 0), (0, LANES - D_IDX))).reshape(d, H_IDX * LANES)
    w2 = jnp.concatenate([part(3), part(4), part(5), qi, _pad_cols(part(7), LANES), _pad_cols(part(8), LANES)], 1)
    n_cb = CONV_W // tc
    w3 = jnp.stack([part(9).reshape(d, n_cb, tc), part(10).reshape(d, n_cb, tc), part(11).reshape(d, n_cb, tc)],
                   axis=2).reshape(d, 3 * CONV_W)
    wg = part(12).reshape(d, 3, d).transpose(1, 0, 2)
    wuq = jnp.pad(w_uq.reshape(Q_RANK, H_MLA, NOPE + ROPE_MLA),
                  ((0, 0), (0, 0), (0, 2 * LANES - NOPE - ROPE_MLA))).reshape(Q_RANK, H_MLA * 2 * LANES)
    wbr = jnp.stack([w_a, w_b, w_c], 0)
    cast = lambda a: a.astype(CDT)
    return cast(w1), cast(w2), cast(w3), cast(wg), cast(wuq), cast(w_ukv), cast(wbr)


def kernel(x, c, positions, ada_w, ada_b, ln1_g, ln1_b, ln2_g, ln2_b, w_in, mla_q_norm, mla_kv_norm, w_uq, w_ukv,
           conv_w, conv_b, w_branch_a, w_branch_b, w_branch_c, w_o, ffn_w_gate, ffn_w_up, ffn_w_down, router_w,
           moe_w_gate, moe_w_up, moe_w_down):
    bsz, s_len, d = x.shape
    n_tok = bsz * s_len
    depth = ada_w.shape[0]
    conv_tc = 512

    cm, sm = _rope_tables(positions, ROPE_MLA, 0.0)
    cd, sd = _rope_tables(positions, ROT_DSA, 1.0)
    ci, si = _rope_tables(positions, ROT_IDX, 1.0)

    mod = _ada(c, ada_w, ada_b)
    vec = lambda a: a.reshape(1, d)

    h = None
    for i in range(depth):
        sh_m, sc_m, g_m, sh_f, sc_f, g_f = [mod[i, :, j * d:(j + 1) * d].reshape(bsz, 1, d) for j in range(6)]
        if h is None:
            h = _modulate(x, sc_m, sh_m, CDT)
        w1, w2, w3, wg, wuq, wukv, wbr = _mixer_weights(w_in[i], w_uq[i], w_ukv[i], w_branch_a[i], w_branch_b[i],
                                                       w_branch_c[i], conv_tc)

        qa, ka, va = _mla_prep(h, w1, mla_q_norm[i].reshape(1, -1), mla_kv_norm[i].reshape(1, -1), wuq, wukv, cm, sm)
        ya = _mla_attention(qa, ka, va)

        qd, kd, vd, qi, ki, wi = _dsa_prep(h, w2, cd, sd, ci, si)
        yb = _dsa_attention(qi, ki, wi, qd, kd, vd)

        yc = _conv_branch(h, w3, conv_w[i], conv_b[i], conv_tc)

        merged = _merge(h.reshape(n_tok, d), ya.reshape(n_tok, -1), yb.reshape(n_tok, -1), yc.reshape(n_tok, -1),
                        wg, wbr)
        moe_layer = i % 2 == 1
        x, h = _out_ln(merged.reshape(bsz, s_len, d), w_o[i].astype(CDT), x, g_m, vec(ln1_g[i]), vec(ln1_b[i]),
                       sc_f, sh_f, F32 if moe_layer else CDT)

        j = i // 2
        if i + 1 < depth:
            sh_n, sc_n = [mod[i + 1, :, q * d:(q + 1) * d].reshape(bsz, 1, d) for q in range(2)]
        else:
            sh_n, sc_n = jnp.zeros((bsz, 1, d), F32), jnp.zeros((bsz, 1, d), F32)
        if not moe_layer:
            x, h = _ffn_ln(h, ffn_w_gate[j].astype(CDT), ffn_w_up[j].astype(CDT), ffn_w_down[j].astype(CDT), x, g_f,
                           vec(ln2_g[i]), vec(ln2_b[i]), sc_n, sh_n)
        else:
            ht = h.reshape(n_tok, d)
            top_e, top_g = _router(ht, _pad_cols(router_w[j], LANES))
            pos, row_tok, grp_e, blk_valid = _routing_tables(top_e[:, :TOP_K], n_tok)
            y_rows = _moe_ffn(ht, row_tok, grp_e, blk_valid, moe_w_gate[j], moe_w_up[j], moe_w_down[j])
            x = _combine_ln(y_rows, pos, top_g, x, g_f, vec(ln2_g[i]), vec(ln2_b[i]))
            h = None if i + 1 >= depth else _modulate(x, sc_n, sh_n, CDT)
    return x
```

```python
import functools

import jax
import jax.numpy as jnp
import numpy as np
from jax import lax
from jax.experimental import pallas as pl
from jax.experimental.pallas import tpu as pltpu

F32 = jnp.float32
CDT = jnp.bfloat16

D = 2048
DEPTH = 2
H_MLA, NOPE, ROPE_MLA, V_MLA = 8, 128, 64, 128
Q_RANK, KV_RANK = 512, 256
H_DSA, G_DSA, HD_DSA = 8, 2, 128
ROT_DSA = HD_DSA // 4
H_IDX, D_IDX = 8, 64
ROT_IDX = D_IDX // 4
TOPK_MAX = 256
CONV_W, CONV_K = 1024, 3
THETA = 500000.0
N_EXPERTS, TOP_K = 8, 2
ALPHA = (2 * DEPTH) ** 0.25
LN_EPS = 1e-5
RMS_EPS = 1e-6
SPLITS = (Q_RANK, KV_RANK, ROPE_MLA, H_DSA * HD_DSA, G_DSA * HD_DSA, G_DSA * HD_DSA,
          H_IDX * D_IDX, D_IDX, H_IDX, CONV_W, CONV_W, CONV_W, 3 * D)

LANES = 128
NEG = -1e30
LOG2E = 1.4426950408889634
VMEM_LIMIT = 56 * 1024 * 1024

MOE_BLK = 512
MOE_GROUP = 2
MLA_BLK = 512


def _params(sem, vmem=VMEM_LIMIT):
    return pltpu.CompilerParams(dimension_semantics=sem, vmem_limit_bytes=vmem)


def _sigmoid(x):
    return 1.0 / (1.0 + jnp.exp(-x))


def _dot(a, b):
    return jnp.dot(a, b, preferred_element_type=F32)


def _dot_t(a, b):
    return lax.dot_general(a, b, (((1,), (1,)), ((), ())), preferred_element_type=F32)


def _layer_norm(y, g, b):
    mu = jnp.mean(y, -1, keepdims=True)
    d = y - mu
    var = jnp.mean(d * d, -1, keepdims=True)
    return d * lax.rsqrt(var + LN_EPS) * g + b


def _rope(y, c, s, half):
    lane = lax.broadcasted_iota(jnp.int32, y.shape, 1)
    swapped = jnp.where(lane < half, pltpu.roll(y, LANES - half, 1), pltpu.roll(y, half, 1))
    return y * c + swapped * s


def _ada_body(c_ref, w_ref, b_ref, o_ref):
    c = c_ref[...]
    ca = (c * _sigmoid(c)).astype(CDT)
    o_ref[...] = _dot(ca, w_ref[...].astype(CDT)) + b_ref[...]


def _ada(c, ada_w, ada_b):
    depth, d, n = ada_w.shape
    bsz = c.shape[0]
    rows = 8
    cp = jnp.zeros((rows, d), F32).at[:bsz].set(c)
    tn = 1024
    out = pl.pallas_call(
        _ada_body,
        grid=(depth, n // tn),
        in_specs=[pl.BlockSpec((rows, d), lambda l, j: (0, 0)),
                  pl.BlockSpec((None, d, tn), lambda l, j: (l, 0, j)),
                  pl.BlockSpec((None, 1, tn), lambda l, j: (l, 0, j))],
        out_specs=pl.BlockSpec((None, rows, tn), lambda l, j: (l, 0, j)),
        out_shape=jax.ShapeDtypeStruct((depth, rows, n), F32),
        compiler_params=_params(("arbitrary", "arbitrary")),
        name="ada_mod",
    )(cp, ada_w, ada_b.reshape(depth, 1, n))
    return out[:, :bsz]


def _mod_body(x_ref, sc_ref, sh_ref, o_ref):
    o_ref[...] = (x_ref[...] * (1.0 + sc_ref[...]) + sh_ref[...]).astype(o_ref.dtype)


def _modulate(x, sc, sh, out_dtype):
    bsz, s_len, d = x.shape
    tm = 512
    vec = pl.BlockSpec((None, 1, d), lambda b, i: (b, 0, 0))
    return pl.pallas_call(
        _mod_body,
        grid=(bsz, s_len // tm),
        in_specs=[pl.BlockSpec((None, tm, d), lambda b, i: (b, i, 0)), vec, vec],
        out_specs=pl.BlockSpec((None, tm, d), lambda b, i: (b, i, 0)),
        out_shape=jax.ShapeDtypeStruct(x.shape, out_dtype),
        compiler_params=_params(("arbitrary", "arbitrary")),
        name="modulate",
    )(x, sc, sh)


def _mla_prep_body(h_ref, w1_ref, qg_ref, kg_ref, wuq_ref, wukv_ref, c_ref, s_ref, q_ref, k_ref, v_ref):
    z = _dot(h_ref[...], w1_ref[...])
    cq = z[:, :Q_RANK]
    ckv = z[:, Q_RANK:Q_RANK + KV_RANK]
    kr = z[:, Q_RANK + KV_RANK:]
    nq = (cq * lax.rsqrt(jnp.mean(cq * cq, -1, keepdims=True) + RMS_EPS) * qg_ref[...]).astype(CDT)
    nkv = (ckv * lax.rsqrt(jnp.mean(ckv * ckv, -1, keepdims=True) + RMS_EPS) * kg_ref[...]).astype(CDT)
    qa = _dot(nq, wuq_ref[...])
    kva = _dot(nkv, wukv_ref[...])
    c = c_ref[...]
    s = s_ref[...]
    scale = (NOPE + ROPE_MLA) ** -0.5 * LOG2E
    krot_t = _rope(kr, c, s, ROPE_MLA // 2).T.astype(CDT)
    lane = lax.broadcasted_iota(jnp.int32, (h_ref.shape[0], LANES), 1)
    ones_col = jnp.where(lane == 0, 1.0, 0.0).astype(CDT)
    for h in range(H_MLA):
        lo = h * 2 * LANES
        q_ref[h, :, 0:LANES] = (qa[:, lo:lo + LANES] * scale).astype(CDT)
        q_ref[h, :, LANES:2 * LANES] = (_rope(qa[:, lo + LANES:lo + 2 * LANES], c, s, ROPE_MLA // 2)
                                        * scale).astype(CDT)
        k_ref[h, 0:LANES, :] = kva[:, lo:lo + LANES].T.astype(CDT)
        k_ref[h, LANES:2 * LANES, :] = krot_t
        v_ref[h, :, 0:LANES] = kva[:, lo + LANES:lo + 2 * LANES].astype(CDT)
        v_ref[h, :, LANES:2 * LANES] = ones_col


def _mla_prep(h, w1, qg, kg, wuq, wukv, ctab, stab):
    bsz, s_len, d = h.shape
    tm = min(MLA_BLK, s_len)
    full = lambda a: pl.BlockSpec(a.shape, lambda b, i: (0,) * a.ndim)
    row = lambda w: pl.BlockSpec((None, tm, w), lambda b, i: (b, i, 0))
    head = lambda w: pl.BlockSpec((None, H_MLA, tm, w), lambda b, i: (b, 0, i, 0))
    return pl.pallas_call(
        _mla_prep_body,
        grid=(bsz, s_len // tm),
        in_specs=[row(d), full(w1), full(qg), full(kg), full(wuq), full(wukv), row(LANES), row(LANES)],
        out_specs=[head(2 * LANES),
                   pl.BlockSpec((None, H_MLA, None, 2 * LANES, tm), lambda b, i: (b, 0, i, 0, 0)),
                   head(2 * LANES)],
        out_shape=[jax.ShapeDtypeStruct((bsz, H_MLA, s_len, 2 * LANES), CDT),
                   jax.ShapeDtypeStruct((bsz, H_MLA, s_len // tm, 2 * LANES, tm), CDT),
                   jax.ShapeDtypeStruct((bsz, H_MLA, s_len, 2 * LANES), CDT)],
        compiler_params=_params(("arbitrary", "arbitrary")),
        name="mla_prep",
    )(h, w1, qg, kg, wuq, wukv, ctab, stab)


def _flash_body(q_ref, k_ref, v_ref, o_ref, s_ref, mx_ref, m_ref, acc_ref, *, blk):
    qi = pl.program_id(2)
    dv = V_MLA

    def scores(hd, j, masked):
        s = _dot(q_ref[hd], k_ref[hd, j])
        if masked:
            row = lax.broadcasted_iota(jnp.int32, s.shape, 0)
            col = lax.broadcasted_iota(jnp.int32, s.shape, 1)
            s = jnp.where(col <= row, s, NEG)
        return s, jnp.max(s, -1, keepdims=True)

    def put(hd, sv):
        s_ref[hd], mx_ref[hd] = sv

    def accumulate(hd, j):
        off = pl.multiple_of(j * blk, blk)
        m_old = m_ref[hd]
        m_new = jnp.maximum(m_old, mx_ref[hd])
        alpha = jnp.exp2(m_old - m_new)
        p = jnp.exp2((s_ref[hd] - m_new).astype(CDT))
        acc_ref[hd] = alpha * acc_ref[hd] + _dot(p, v_ref[hd, pl.ds(off, blk), :])
        m_ref[hd] = m_new

    m_ref[...] = jnp.full(m_ref.shape, NEG, F32)
    acc_ref[...] = jnp.zeros(acc_ref.shape, F32)
    put(0, scores(0, qi, True))
    put(1, scores(1, qi, True))

    def body(n, _):
        j_prev = jnp.where(n == 1, qi, n - 2)
        accumulate(0, j_prev)
        accumulate(1, j_prev)
        put(0, scores(0, n - 1, False))
        put(1, scores(1, n - 1, False))
        return 0

    lax.fori_loop(1, qi + 1, body, 0)
    j_last = jnp.where(qi == 0, qi, qi - 1)
    accumulate(0, j_last)
    accumulate(1, j_last)
    for hd in range(2):
        acc = acc_ref[hd]
        o_ref[:, hd * dv:(hd + 1) * dv] = (acc[:, :dv] / acc[:, dv:dv + 1]).astype(o_ref.dtype)


def _mla_attention(q, k, v):
    bsz, nh, s_len, dq = q.shape
    dve = v.shape[-1]
    dv = V_MLA
    blk = min(MLA_BLK, s_len)
    return pl.pallas_call(
        functools.partial(_flash_body, blk=blk),
        grid=(bsz, nh // 2, s_len // blk),
        in_specs=[pl.BlockSpec((None, 2, blk, dq), lambda b, h, i: (b, h, i, 0)),
                  pl.BlockSpec((None, 2, s_len // blk, dq, blk), lambda b, h, i: (b, h, 0, 0, 0)),
                  pl.BlockSpec((None, 2, s_len, dve), lambda b, h, i: (b, h, 0, 0))],
        out_specs=pl.BlockSpec((None, blk, 2 * dv), lambda b, h, i: (b, i, h)),
        out_shape=jax.ShapeDtypeStruct((bsz, s_len, nh * dv), CDT),
        scratch_shapes=[pltpu.VMEM((2, blk, blk), F32), pltpu.VMEM((2, blk, 1), F32),
                        pltpu.VMEM((2, blk, 1), F32), pltpu.VMEM((2, blk, dve), F32)],
        compiler_params=_params(("arbitrary", "arbitrary", "arbitrary")),
        name="mla_attention",
    )(q, k, v)


DSA_CHUNK = 512
DSA_TQ = 128
VT_PAD = 16
VT_ROWS = HD_DSA + VT_PAD


def _dsa_prep_body(h_ref, w2_ref, cd_ref, sd_ref, ci_ref, si_ref, q_ref, k_ref, vt_ref, qi_ref, ki_ref, wit_ref):
    z = _dot(h_ref[...], w2_ref[...])
    cd, sd, ci, si = cd_ref[...], sd_ref[...], ci_ref[...], si_ref[...]
    col = lambda j: z[:, j * LANES:(j + 1) * LANES]
    for h in range(H_DSA):
        q_ref[h] = (_rope(col(h), cd, sd, ROT_DSA // 2) * (HD_DSA ** -0.5 * LOG2E)).T.astype(CDT)
    for g in range(G_DSA):
        k_ref[:, g * LANES:(g + 1) * LANES] = _rope(col(H_DSA + g), cd, sd, ROT_DSA // 2).astype(CDT)
    base = H_DSA + G_DSA
    sub = lax.broadcasted_iota(jnp.int32, (VT_PAD, z.shape[0]), 0)
    ones_row = jnp.where(sub == 0, 1.0, 0.0).astype(CDT)
    for g in range(G_DSA):
        vt_ref[g * VT_ROWS:g * VT_ROWS + LANES, :] = col(base + g).T.astype(CDT)
        vt_ref[g * VT_ROWS + LANES:(g + 1) * VT_ROWS, :] = ones_row
    base += G_DSA
    for h in range(H_IDX):
        qi_ref[h] = (_rope(col(base + h), ci, si, ROT_IDX // 2) * D_IDX ** -0.5).T.astype(CDT)
    base += H_IDX
    ki_ref[...] = _rope(col(base), ci, si, ROT_IDX // 2).astype(CDT)
    wit_ref[...] = col(base + 1).T[0:H_IDX, :] * H_IDX ** -0.5


def _dsa_prep(h, w2, cd, sd, ci, si):
    bsz, s_len, d = h.shape
    tm = min(DSA_CHUNK, s_len)
    row = lambda w: pl.BlockSpec((None, tm, w), lambda b, i: (b, i, 0))
    return pl.pallas_call(
        _dsa_prep_body,
        grid=(bsz, s_len // tm),
        in_specs=[row(d), pl.BlockSpec(w2.shape, lambda b, i: (0, 0)), row(LANES), row(LANES), row(LANES),
                  row(LANES)],
        out_specs=[pl.BlockSpec((None, H_DSA, LANES, tm), lambda b, i: (b, 0, 0, i)), row(G_DSA * LANES),
                   pl.BlockSpec((None, None, G_DSA * VT_ROWS, tm), lambda b, i: (b, i, 0, 0)),
                   pl.BlockSpec((None, H_IDX, LANES, tm), lambda b, i: (b, 0, 0, i)), row(LANES),
                   pl.BlockSpec((None, H_IDX, tm), lambda b, i: (b, 0, i))],
        out_shape=[jax.ShapeDtypeStruct((bsz, H_DSA, LANES, s_len), CDT),
                   jax.ShapeDtypeStruct((bsz, s_len, G_DSA * LANES), CDT),
                   jax.ShapeDtypeStruct((bsz, s_len // tm, G_DSA * VT_ROWS, tm), CDT),
                   jax.ShapeDtypeStruct((bsz, H_IDX, LANES, s_len), CDT),
                   jax.ShapeDtypeStruct((bsz, s_len, LANES), CDT),
                   jax.ShapeDtypeStruct((bsz, H_IDX, s_len), F32)],
        compiler_params=_params(("arbitrary", "arbitrary")),
        name="dsa_prep",
    )(h, w2, cd, sd, ci, si)


FOLD_ROWS = 64


def _fold(x):
    return jnp.sum(x.reshape(x.shape[0] // FOLD_ROWS, FOLD_ROWS, x.shape[1]), axis=0)


def _dsa_body(qi_ref, ki_ref, wit_ref, q_ref, k_ref, vt_ref, tri_ref, o_ref,
              key_ref, sel_ref, qia_ref, qg_ref, s_ref, mx_ref, m_ref, acc_ref, *, tq, tkc, n_sel):
    i = pl.program_id(1)
    q0 = i * tq
    nkc = (q0 + tq + tkc - 1) // tkc
    n_rep = H_DSA // G_DSA
    key_neg_inf = jnp.int32(-8388608) ^ jnp.int32(0x7FFFFFFF)
    qpos = q0 + lax.broadcasted_iota(jnp.int32, (tkc, tq), 1)

    for h in range(H_IDX):
        qia_ref[:, h * tq:(h + 1) * tq] = qi_ref[h]
    for g in range(G_DSA):
        for r in range(n_rep):
            qg_ref[g, :, r * tq:(r + 1) * tq] = q_ref[g * n_rep + r]

    def score_chunk(c, _):
        off = pl.multiple_of(c * tkc, tkc)
        kc = ki_ref[pl.ds(off, tkc), :]
        w = wit_ref[...]
        lg = _dot(kc, qia_ref[...])
        sc = jnp.zeros((tkc, tq), F32)
        for h in range(H_IDX):
            sc = sc + w[h:h + 1, :] * jnp.maximum(lg[:, h * tq:(h + 1) * tq], 0.0)
        kpos = off + lax.broadcasted_iota(jnp.int32, (tkc, tq), 0)
        sc = jnp.where(kpos <= qpos, sc, -jnp.inf)
        bits = pltpu.bitcast(sc, jnp.int32)
        key_ref[c] = jnp.where(bits >= 0, bits, bits ^ jnp.int32(0x7FFFFFFF))
        return 0

    lax.fori_loop(0, nkc, score_chunk, 0)

    def count(pred):
        def body(c, cnt):
            return cnt + _fold(jnp.where(pred(key_ref[c]), 1.0, 0.0))
        part = lax.fori_loop(0, nkc, body, jnp.zeros((FOLD_ROWS, tq), F32))
        return jnp.sum(part, axis=0, keepdims=True)

    def bit_step(bi, thr):
        cand = thr + lax.shift_left(jnp.int32(1), 31 - bi)
        cnt = count(lambda kk: kk >= cand)
        return jnp.where(cnt >= n_sel, cand, thr)

    thr = lax.fori_loop(0, 32, bit_step, jnp.full((1, tq), jnp.iinfo(jnp.int32).min, jnp.int32))

    cnt_gt = count(lambda kk: kk > thr)
    cnt_eq = count(lambda kk: kk == thr)
    need = n_sel - cnt_gt
    excess = jnp.where(thr > key_neg_inf, cnt_eq - need, 0.0)

    def mask_plain():
        def body(c, _):
            kk = key_ref[c]
            sel_ref[c] = jnp.where(kk >= thr, jnp.where(kk > key_neg_inf, 0.0, NEG), NEG)
            return 0
        lax.fori_loop(0, nkc, body, 0)

    def mask_ranked():
        def body(c, seen):
            kk = key_ref[c]
            tie = kk == thr
            pre = _dot(tri_ref[...], jnp.where(tie, 1.0, 0.0).astype(CDT))
            rank = seen + pre
            keep_tie = jnp.where(tie, jnp.where(rank <= need, 0.0, NEG), NEG)
            keep = jnp.where(kk > thr, 0.0, keep_tie)
            sel_ref[c] = jnp.where(kk > key_neg_inf, keep, NEG)
            return seen + pre[tkc - 1:tkc, :]
        lax.fori_loop(0, nkc, body, jnp.zeros((1, tq), F32))

    lax.cond(jnp.max(excess) > 0.0, mask_ranked, mask_plain)

    m_ref[...] = jnp.full(m_ref.shape, NEG, F32)
    acc_ref[...] = jnp.zeros(acc_ref.shape, F32)

    def scores(g, c):
        off = pl.multiple_of(c * tkc, tkc)
        msk = sel_ref[c]
        msk = jnp.concatenate([msk] * n_rep, axis=1)
        kg = k_ref[pl.ds(off, tkc), g * LANES:(g + 1) * LANES]
        s = _dot(kg, qg_ref[g]) + msk
        return s, jnp.max(s, axis=0, keepdims=True)

    def put(g, sv):
        s_ref[g], mx_ref[g] = sv

    def accumulate(g, c):
        m_old = m_ref[g]
        m_new = jnp.maximum(m_old, mx_ref[g])
        alpha = jnp.exp2(m_old - m_new)
        p = jnp.exp2((s_ref[g] - m_new).astype(CDT))
        acc_ref[g] = alpha * acc_ref[g] + _dot(vt_ref[c, g * VT_ROWS:(g + 1) * VT_ROWS, :], p)
        m_ref[g] = m_new

    put(0, scores(0, 0))
    put(1, scores(1, 0))

    def att_chunk(c, _):
        sv0 = scores(0, c)
        sv1 = scores(1, c)
        accumulate(0, c - 1)
        accumulate(1, c - 1)
        put(0, sv0)
        put(1, sv1)
        return 0

    lax.fori_loop(1, nkc, att_chunk, 0)
    accumulate(0, nkc - 1)
    accumulate(1, nkc - 1)
    for g in range(G_DSA):
        acc = acc_ref[g]
        out = acc[0:HD_DSA, :] / acc[HD_DSA:HD_DSA + 1, :]
        for r in range(n_rep):
            hh = g * n_rep + r
            o_ref[:, hh * LANES:(hh + 1) * LANES] = out[:, r * tq:(r + 1) * tq].T.astype(o_ref.dtype)


def _dsa_attention(qi, ki, wit, q, k, vt):
    bsz, s_len, _ = k.shape
    tq = min(DSA_TQ, s_len)
    tkc = min(DSA_CHUNK, s_len)
    n_chunks = s_len // tkc
    n_sel = min(TOPK_MAX, s_len // 4)
    n_rep = H_DSA // G_DSA
    tri =(lax.broadcasted_iota(jnp.int32, (tkc, tkc), 1)
           <= lax.broadcasted_iota(jnp.int32, (tkc, tkc), 0)).astype(CDT)
    row = lambda w: pl.BlockSpec((None, tq, w), lambda b, i: (b, i, 0))
    seq = lambda w: pl.BlockSpec((None, s_len, w), lambda b, i: (b, 0, 0))
    return pl.pallas_call(
        functools.partial(_dsa_body, tq=tq, tkc=tkc, n_sel=n_sel),
        grid=(bsz, s_len // tq),
        in_specs=[pl.BlockSpec((None, H_IDX, LANES, tq), lambda b, i: (b, 0, 0, i)), seq(LANES),
                  pl.BlockSpec((None, H_IDX, tq), lambda b, i: (b, 0, i)),
                  pl.BlockSpec((None, H_DSA, LANES, tq), lambda b, i: (b, 0, 0, i)), seq(G_DSA * LANES),
                  pl.BlockSpec((None, n_chunks, G_DSA * VT_ROWS, tkc), lambda b, i: (b, 0, 0, 0)),
                  pl.BlockSpec((tkc, tkc), lambda b, i: (0, 0))],
        out_specs=row(H_DSA * LANES),
        out_shape=jax.ShapeDtypeStruct((bsz, s_len, H_DSA * LANES), CDT),
        scratch_shapes=[pltpu.VMEM((n_chunks, tkc, tq), jnp.int32),
                        pltpu.VMEM((n_chunks, tkc, tq), F32),
                        pltpu.VMEM((LANES, H_IDX * tq), CDT),
                        pltpu.VMEM((G_DSA, LANES, n_rep * tq), CDT),
                        pltpu.VMEM((G_DSA, tkc, n_rep * tq), F32),
                        pltpu.VMEM((G_DSA, 1, n_rep * tq), F32),
                        pltpu.VMEM((G_DSA, 1, n_rep * tq), F32),
                        pltpu.VMEM((G_DSA, VT_ROWS, n_rep * tq), F32)],
        compiler_params=_params(("arbitrary", "arbitrary")),
        name="dsa_attention",
    )(qi, ki, wit, q, k, vt, tri)


def _conv_body(h_ref, w_ref, cw_ref, cb_ref, o_ref, tail_ref, *, tm, tc):
    @pl.when(pl.program_id(2) == 0)
    def _():
        tail_ref[...] = jnp.zeros(tail_ref.shape, F32)

    z = _dot(h_ref[...], w_ref[...])
    gb = z[:, :tc]
    u = z[:, tc:2 * tc] * z[:, 2 * tc:]
    prev = tail_ref[...]
    row = lax.broadcasted_iota(jnp.int32, (tm, tc), 0)
    u1 = jnp.where(row == 0, prev[7:8, :], pltpu.roll(u, 1, 0))
    u2 = jnp.where(row == 0, prev[6:7, :], jnp.where(row == 1, prev[7:8, :], pltpu.roll(u, 2, 0)))
    cw = cw_ref[...]
    y = cw[0:1, :] * u2 + cw[1:2, :] * u1 + cw[2:3, :] * u + cb_ref[...]
    o_ref[...] = (gb * y).astype(o_ref.dtype)
    tail_ref[...] = u[tm - 8:tm, :]


def _conv_branch(h, w3, conv_w, conv_b, tc):
    bsz, s_len, d = h.shape
    tm = 512
    n_cb = CONV_W // tc
    return pl.pallas_call(
        functools.partial(_conv_body, tm=tm, tc=tc),
        grid=(n_cb, bsz, s_len // tm),
        in_specs=[pl.BlockSpec((None, tm, d), lambda cb, b, i: (b, i, 0)),
                  pl.BlockSpec((d, 3 * tc), lambda cb, b, i: (0, cb)),
                  pl.BlockSpec((CONV_K, tc), lambda cb, b, i: (0, cb)),
                  pl.BlockSpec((1, tc), lambda cb, b, i: (0, cb))],
        out_specs=pl.BlockSpec((None, tm, tc), lambda cb, b, i: (b, i, cb)),
        out_shape=jax.ShapeDtypeStruct((bsz, s_len, CONV_W), CDT),
        scratch_shapes=[pltpu.VMEM((8, tc), F32)],
        compiler_params=_params(("arbitrary", "arbitrary", "arbitrary")),
        name="conv_branch",
    )(h, w3, conv_w, conv_b.reshape(1, CONV_W))


def _merge_body(h_ref, ya_ref, yb_ref, yc_ref, wg_ref, wbr_ref, o_ref):
    h = h_ref[...]
    acc = None
    for j, y_ref in enumerate((ya_ref, yb_ref, yc_ref)):
        term = _sigmoid(_dot(h, wg_ref[j])) * _dot(y_ref[...], wbr_ref[j])
        acc = term if acc is None else acc + term
    o_ref[...] = acc.astype(o_ref.dtype)


def _merge(h, ya, yb, yc, wg, wbr):
    t, d = h.shape
    tm, tn = 512, 512
    yw = ya.shape[1]
    row = lambda w: pl.BlockSpec((tm, w), lambda j, i: (i, 0))
    return pl.pallas_call(
        _merge_body,
        grid=(d // tn, t // tm),
        in_specs=[row(d), row(yw), row(yw), row(yw),
                  pl.BlockSpec((3, d, tn), lambda j, i: (0, 0, j)),
                  pl.BlockSpec((3, yw, tn), lambda j, i: (0, 0, j))],
        out_specs=pl.BlockSpec((tm, tn), lambda j, i: (i, j)),
        out_shape=jax.ShapeDtypeStruct((t, d), CDT),
        compiler_params=_params(("arbitrary", "arbitrary")),
        name="gated_merge",
    )(h, ya, yb, yc, wg, wbr)


def _out_ln_body(m_ref, wo_ref, x_ref, gm_ref, lg_ref, lb_ref, sc_ref, sh_ref, x1_ref, h_ref):
    y = ALPHA * x_ref[...] + gm_ref[...] * _dot(m_ref[...], wo_ref[...])
    x1 = _layer_norm(y, lg_ref[...], lb_ref[...])
    x1_ref[...] = x1
    h_ref[...] = (x1 * (1.0 + sc_ref[...]) + sh_ref[...]).astype(h_ref.dtype)


def _out_ln(merged, wo, x, gm, lg, lb, sc, sh, h_dtype):
    bsz, s_len, d = x.shape
    tm = 256
    row = pl.BlockSpec((None, tm, d), lambda b, i: (b, i, 0))
    per_b = pl.BlockSpec((None, 1, d), lambda b, i: (b, 0, 0))
    vec = pl.BlockSpec((1, d), lambda b, i: (0, 0))
    return pl.pallas_call(
        _out_ln_body,
        grid=(bsz, s_len // tm),
        in_specs=[row, pl.BlockSpec((d, d), lambda b, i: (0, 0)), row, per_b, vec, vec, per_b, per_b],
        out_specs=[row, row],
        out_shape=[jax.ShapeDtypeStruct(x.shape, F32), jax.ShapeDtypeStruct(x.shape, h_dtype)],
        compiler_params=_params(("arbitrary", "arbitrary")),
        name="out_proj_ln",
    )(merged, wo, x, gm, lg, lb, sc, sh)


def _ffn_body(h_ref, wg_ref, wu_ref, wd_ref, x_ref, gf_ref, lg_ref, lb_ref, sc_ref, sh_ref, x2_ref, h2_ref,
              acc_ref):
    k = pl.program_id(2)

    @pl.when(k == 0)
    def _():
        acc_ref[...] = jnp.zeros(acc_ref.shape, F32)

    h = h_ref[...]
    a = _dot(h, wg_ref[...])
    act = (a * _sigmoid(a) * _dot(h, wu_ref[...])).astype(CDT)
    acc_ref[...] += _dot(act, wd_ref[...])

    @pl.when(k == pl.num_programs(2) - 1)
    def _():
        y = ALPHA * x_ref[...] + gf_ref[...] * acc_ref[...]
        x2 = _layer_norm(y, lg_ref[...], lb_ref[...])
        x2_ref[...] = x2
        h2_ref[...] = (x2 * (1.0 + sc_ref[...]) + sh_ref[...]).astype(h2_ref.dtype)


def _ffn_ln(h, wg, wu, wd, x, gf, lg, lb, sc, sh):
    bsz, s_len, d = x.shape
    f = wg.shape[1]
    tm, tf = 512, 512
    row = pl.BlockSpec((None, tm, d), lambda b, i, k: (b, i, 0))
    per_b = pl.BlockSpec((None, 1, d), lambda b, i, k: (b, 0, 0))
    vec = pl.BlockSpec((1, d), lambda b, i, k: (0, 0))
    return pl.pallas_call(
        _ffn_body,
        grid=(bsz, s_len // tm, f // tf),
        in_specs=[row, pl.BlockSpec((d, tf), lambda b, i, k: (0, k)), pl.BlockSpec((d, tf), lambda b, i, k: (0, k)),
                  pl.BlockSpec((tf, d), lambda b, i, k: (k, 0)), row, per_b, vec, vec, per_b, per_b],
        out_specs=[row, row],
        out_shape=[jax.ShapeDtypeStruct(x.shape, F32), jax.ShapeDtypeStruct(x.shape, CDT)],
        scratch_shapes=[pltpu.VMEM((tm, d), F32)],
        compiler_params=_params(("arbitrary", "arbitrary", "arbitrary")),
        name="ffn_ln",
    )(h, wg, wu, wd, x, gf, lg, lb, sc, sh)


def _router_body(h_ref, wr_ref, e_ref, g_ref):
    lg = jnp.dot(h_ref[...], wr_ref[...], preferred_element_type=F32, precision=lax.Precision.HIGHEST)
    lane = lax.broadcasted_iota(jnp.int32, lg.shape, 1)
    lanef = lane.astype(F32)
    lg = jnp.where(lane < N_EXPERTS, lg, -jnp.inf)
    m1 = jnp.max(lg, -1, keepdims=True)
    i1 = jnp.min(jnp.where(lg == m1, lanef, float(LANES)), -1, keepdims=True)
    lg2 = jnp.where(lanef == i1, -jnp.inf, lg)
    m2 = jnp.max(lg2, -1, keepdims=True)
    i2 = jnp.min(jnp.where(lg2 == m2, lanef, float(LANES)), -1, keepdims=True)
    e = jnp.exp(m2 - m1)
    g1 = 1.0 / (1.0 + e)
    g2 = e / (1.0 + e)
    e_ref[...] = jnp.where(lane == 0, i1, jnp.where(lane == 1, i2, 0.0)).astype(jnp.int32)
    g_ref[...] = jnp.where(lane == 0, g1, jnp.where(lane == 1, g2, 0.0))


def _router(h, wr):
    t, d = h.shape
    tm = 512
    return pl.pallas_call(
        _router_body,
        grid=(t // tm,),
        in_specs=[pl.BlockSpec((tm, d), lambda i: (i, 0)), pl.BlockSpec((d, LANES), lambda i: (0, 0))],
        out_specs=[pl.BlockSpec((tm, LANES), lambda i: (i, 0)), pl.BlockSpec((tm, LANES), lambda i: (i, 0))],
        out_shape=[jax.ShapeDtypeStruct((t, LANES), jnp.int32), jax.ShapeDtypeStruct((t, LANES), F32)],
        compiler_params=_params(("arbitrary",)),
        name="moe_router",
    )(h, wr)


def _moe_body(tok_ref, ge_ref, bv_ref, h_hbm, wg_ref, wu_ref, wd_ref, y_hbm, xbuf, xs, y_ref, sem, out_sem, *, blk):
    i = pl.program_id(0)
    k = pl.program_id(1)
    ng = pl.num_programs(0)
    nk = pl.num_programs(1)
    rows = MOE_GROUP * blk

    def out_copy(gi):
        return pltpu.make_async_copy(y_ref, y_hbm.at[pl.ds(pl.multiple_of(gi * rows, rows), rows), :], out_sem)

    def issue(gi):
        def body(r, _):
            pltpu.make_async_copy(h_hbm.at[pl.ds(tok_ref[gi * rows + r], 1), :], xbuf.at[pl.ds(r, 1), :], sem).start()
            return 0
        lax.fori_loop(0, rows, body, 0)

    @pl.when(k == 0)
    def _():
        @pl.when(i == 0)
        def _():
            issue(0)

        pltpu.make_async_copy(h_hbm.at[pl.ds(0, rows), :], xbuf, sem).wait()
        xs[...] = xbuf[...].astype(CDT)

        @pl.when(i + 1 < ng)
        def _():
            issue(i + 1)

        @pl.when(i > 0)
        def _():
            out_copy(i - 1).wait()

        y_ref[...] = jnp.zeros(y_ref.shape, F32)

    def ffn(n_blocks):
        wg = wg_ref[...].astype(CDT)
        wu = wu_ref[...].astype(CDT)
        wd = wd_ref[...].astype(CDT)
        for b in range(n_blocks):
            x = xs[b * blk:(b + 1) * blk, :]
            a = _dot(x, wg)
            act = (a * _sigmoid(a) * _dot(x, wu)).astype(CDT)
            y_ref[b * blk:(b + 1) * blk, :] += _dot(act, wd)

    live = bv_ref[i * MOE_GROUP]
    for b in range(1, MOE_GROUP):
        live = live + bv_ref[i * MOE_GROUP + b]
    for n_blocks in range(1, MOE_GROUP + 1):
        pl.when(live == n_blocks)(functools.partial(ffn, n_blocks))

    @pl.when(k == nk - 1)
    def _():
        out_copy(i).start()

        @pl.when(i == ng - 1)
        def _():
            out_copy(i).wait()


def _moe_ffn(h, row_tok, grp_e, blk_valid, wg, wu, wd):
    t, d = h.shape
    n_rows = row_tok.shape[0]
    blk = MOE_BLK
    rows = MOE_GROUP * blk
    ng = n_rows // rows
    f = wg.shape[2]
    tf = 512
    nk = f // tf
    kk = lambda k, i, bv: jnp.where(bv[i * MOE_GROUP] == 1, k, nk - 1)
    grid_spec = pltpu.PrefetchScalarGridSpec(
        num_scalar_prefetch=3,
        grid=(ng, nk),
        in_specs=[pl.BlockSpec(memory_space=pl.ANY),
                  pl.BlockSpec((None, d, tf), lambda i, k, tok, ge, bv: (ge[i], 0, kk(k, i, bv))),
                  pl.BlockSpec((None, d, tf), lambda i, k, tok, ge, bv: (ge[i], 0, kk(k, i, bv))),
                  pl.BlockSpec((None, tf, d), lambda i, k, tok, ge, bv: (ge[i], kk(k, i, bv), 0))],
        out_specs=pl.BlockSpec(memory_space=pl.ANY),
        scratch_shapes=[pltpu.VMEM((rows, d), F32), pltpu.VMEM((rows, d), CDT), pltpu.VMEM((rows, d), F32),
                        pltpu.SemaphoreType.DMA(()), pltpu.SemaphoreType.DMA(())],
    )
    return pl.pallas_call(
        functools.partial(_moe_body, blk=blk),
        grid_spec=grid_spec,
        out_shape=jax.ShapeDtypeStruct((n_rows, d), F32),
        compiler_params=_params(("arbitrary", "arbitrary")),
        name="moe_grouped_ffn",
    )(row_tok, grp_e, blk_valid, h, wg, wu, wd)


def _combine_body(pos_ref, y_hbm, gate_ref, x_ref, gf_ref, lg_ref, lb_ref, o_ref, ybuf, sem, *, tm):
    i = pl.program_id(0)
    nb = pl.num_programs(0)

    def issue(bi, slot):
        def body(r, _):
            for s in range(TOP_K):
                p = pos_ref[(bi * tm + r) * TOP_K + s]
                pltpu.make_async_copy(y_hbm.at[pl.ds(p, 1), :], ybuf.at[slot, s, pl.ds(r, 1), :],
                                      sem.at[slot]).start()
            return 0
        lax.fori_loop(0, tm, body, 0)

    @pl.when(i == 0)
    def _():
        issue(0, 0)

    slot = i % 2
    for s in range(TOP_K):
        pltpu.make_async_copy(y_hbm.at[pl.ds(0, tm), :], ybuf.at[slot, s], sem.at[slot]).wait()

    @pl.when(i + 1 < nb)
    def _():
        issue(i + 1, (i + 1) % 2)

    gate = gate_ref[...]
    f = ybuf[slot, 0] * gate[:, 0:1] + ybuf[slot, 1] * gate[:, 1:2]
    y = ALPHA * x_ref[...] + gf_ref[...] * f
    o_ref[...] = _layer_norm(y, lg_ref[...], lb_ref[...])


def _combine_ln(y_rows, pos, gate, x, gf, lg, lb):
    bsz, s_len, d = x.shape
    tm = 256
    nsb = s_len // tm
    grid_spec = pltpu.PrefetchScalarGridSpec(
        num_scalar_prefetch=1,
        grid=(bsz * nsb,),
        in_specs=[pl.BlockSpec(memory_space=pl.ANY),
                  pl.BlockSpec((tm, LANES), lambda i, pos: (i, 0)),
                  pl.BlockSpec((None, tm, d), lambda i, pos: (i // nsb, i % nsb, 0)),
                  pl.BlockSpec((None, 1, d), lambda i, pos: (i // nsb, 0, 0)),
                  pl.BlockSpec((1, d), lambda i, pos: (0, 0)),
                  pl.BlockSpec((1, d), lambda i, pos: (0, 0))],
        out_specs=pl.BlockSpec((None, tm, d), lambda i, pos: (i // nsb, i % nsb, 0)),
        scratch_shapes=[pltpu.VMEM((2, TOP_K, tm, d), F32), pltpu.SemaphoreType.DMA((2,))],
    )
    return pl.pallas_call(
        functools.partial(_combine_body, tm=tm),
        grid_spec=grid_spec,
        out_shape=jax.ShapeDtypeStruct(x.shape, F32),
        compiler_params=_params(("arbitrary",)),
        name="moe_combine_ln",
    )(pos, y_rows, gate, x, gf, lg, lb)


def _routing_tables(top_e, n_tok):
    blk = MOE_BLK
    rows = MOE_GROUP * blk
    flat_e = top_e.reshape(-1)
    onehot = (flat_e[:, None] == jnp.arange(N_EXPERTS, dtype=jnp.int32)[None, :]).astype(jnp.int32)
    csum = jnp.cumsum(onehot, axis=0)
    counts = csum[-1]
    rank = jnp.sum(csum * onehot, axis=1) - 1
    padded = (counts + rows - 1) // rows * rows
    pend = jnp.cumsum(padded)
    pstart = pend - padded
    pos = (jnp.sum(pstart[None, :] * onehot, axis=1) + rank).astype(jnp.int32)
    n_rows = n_tok * TOP_K + N_EXPERTS * rows
    flat_tok = jnp.repeat(jnp.arange(n_tok, dtype=jnp.int32), TOP_K)
    row_tok = jnp.zeros((n_rows,), jnp.int32).at[pos].set(flat_tok)
    grp_start = jnp.arange(n_rows // rows, dtype=jnp.int32) * rows
    last_start = jnp.maximum(pend[-1] - rows, 0)
    grp_e = jnp.sum((pend[None, :] <= jnp.minimum(grp_start, last_start)[:, None]).astype(jnp.int32), axis=1)
    grp_e = jnp.minimum(grp_e, N_EXPERTS - 1)
    blk_start = jnp.arange(n_rows // blk, dtype=jnp.int32) * blk
    blk_e = jnp.repeat(grp_e, MOE_GROUP)
    blk_valid = ((blk_start < pend[-1]) & (blk_start - pstart[blk_e] < counts[blk_e])).astype(jnp.int32)
    return pos, row_tok, grp_e, blk_valid


def _rope_tables(positions, rot_dim, pad_value):
    half = rot_dim // 2
    inv = jnp.asarray(THETA ** (-np.arange(0, rot_dim, 2, dtype=np.float32) / np.float32(rot_dim)), F32)
    ang = positions.astype(F32)[..., None] * inv
    cos, sin = jnp.cos(ang), jnp.sin(ang)
    rest = positions.shape + (LANES - 2 * half,)
    c = jnp.concatenate([cos, cos, jnp.full(rest, pad_value, F32)], -1)
    s = jnp.concatenate([-sin, sin, jnp.zeros(rest, F32)], -1)
    return c, s


def _pad_cols(w, width):
    return jnp.pad(w, ((0, 0), (0, width - w.shape[1])))


def _mixer_weights(w_in, w_uq, w_ukv, w_a, w_b, w_c, tc):
    offs = [0]
    for n in SPLITS:
        offs.append(offs[-1] + n)
    part = lambda j: w_in[:, offs[j]:offs[j + 1]]
    d = w_in.shape[0]
    w1 = _pad_cols(jnp.concatenate([part(0), part(1), part(2)], 1), Q_RANK + KV_RANK + LANES)
    qi = jnp.pad(part(6).reshape(d, H_IDX, D_IDX), ((0, 0), (0, 0), (0, LANES - D_IDX))).reshape(d, H_IDX * LANES)
    w2 = jnp.concatenate([part(3), part(4), part(5), qi, _pad_cols(part(7), LANES), _pad_cols(part(8), LANES)], 1)
    n_cb = CONV_W // tc
    w3 = jnp.stack([part(9).reshape(d, n_cb, tc), part(10).reshape(d, n_cb, tc), part(11).reshape(d, n_cb, tc)],
                   axis=2).reshape(d, 3 * CONV_W)
    wg = part(12).reshape(d, 3, d).transpose(1, 0, 2)
    wuq = jnp.pad(w_uq.reshape(Q_RANK, H_MLA, NOPE + ROPE_MLA),
                  ((0, 0), (0, 0), (0, 2 * LANES - NOPE - ROPE_MLA))).reshape(Q_RANK, H_MLA * 2 * LANES)
    wbr = jnp.stack([w_a, w_b, w_c], 0)
    cast = lambda a: a.astype(CDT)
    return cast(w1), cast(w2), cast(w3), cast(wg), cast(wuq), cast(w_ukv), cast(wbr)


def kernel(x, c, positions, ada_w, ada_b, ln1_g, ln1_b, ln2_g, ln2_b, w_in, mla_q_norm, mla_kv_norm, w_uq, w_ukv,
           conv_w, conv_b, w_branch_a, w_branch_b, w_branch_c, w_o, ffn_w_gate, ffn_w_up, ffn_w_down, router_w,
           moe_w_gate, moe_w_up, moe_w_down):
    bsz, s_len, d = x.shape
    n_tok = bsz * s_len
    depth = ada_w.shape[0]
    conv_tc = 512

    cm, sm = _rope_tables(positions, ROPE_MLA, 0.0)
    cd, sd = _rope_tables(positions, ROT_DSA, 1.0)
    ci, si = _rope_tables(positions, ROT_IDX, 1.0)

    mod = _ada(c, ada_w, ada_b)
    vec = lambda a: a.reshape(1, d)

    h = None
    for i in range(depth):
        sh_m, sc_m, g_m, sh_f, sc_f, g_f = [mod[i, :, j * d:(j + 1) * d].reshape(bsz, 1, d) for j in range(6)]
        if h is None:
            h = _modulate(x, sc_m, sh_m, CDT)
        w1, w2, w3, wg, wuq, wukv, wbr = _mixer_weights(w_in[i], w_uq[i], w_ukv[i], w_branch_a[i], w_branch_b[i],
                                                       w_branch_c[i], conv_tc)

        qa, ka, va = _mla_prep(h, w1, mla_q_norm[i].reshape(1, -1), mla_kv_norm[i].reshape(1, -1), wuq, wukv, cm, sm)
        ya = _mla_attention(qa, ka, va)

        qd, kd, vd, qi, ki, wi = _dsa_prep(h, w2, cd, sd, ci, si)
        yb = _dsa_attention(qi, ki, wi, qd, kd, vd)

        yc = _conv_branch(h, w3, conv_w[i], conv_b[i], conv_tc)

        merged = _merge(h.reshape(n_tok, d), ya.reshape(n_tok, -1), yb.reshape(n_tok, -1), yc.reshape(n_tok, -1),
                        wg, wbr)
        moe_layer = i % 2 == 1
        x, h = _out_ln(merged.reshape(bsz, s_len, d), w_o[i].astype(CDT), x, g_m, vec(ln1_g[i]), vec(ln1_b[i]),
                       sc_f, sh_f, F32 if moe_layer else CDT)

        j = i // 2
        if i + 1 < depth:
            sh_n, sc_n = [mod[i + 1, :, q * d:(q + 1) * d].reshape(bsz, 1, d) for q in range(2)]
        else:
            sh_n, sc_n = jnp.zeros((bsz, 1, d), F32), jnp.zeros((bsz, 1, d), F32)
        if not moe_layer:
            x, h = _ffn_ln(h, ffn_w_gate[j].astype(CDT), ffn_w_up[j].astype(CDT), ffn_w_down[j].astype(CDT), x, g_f,
                           vec(ln2_g[i]), vec(ln2_b[i]), sc_n, sh_n)
        else:
            ht = h.reshape(n_tok, d)
            top_e, top_g = _router(ht, _pad_cols(router_w[j], LANES))
            pos, row_tok, grp_e, blk_valid = _routing_tables(top_e[:, :TOP_K], n_tok)
            y_rows = _moe_ffn(ht, row_tok, grp_e, blk_valid, moe_w_gate[j], moe_w_up[j], moe_w_down[j])
            x = _combine_ln(y_rows, pos, top_g, x, g_f, vec(ln2_g[i]), vec(ln2_b[i]))
            h = None if i + 1 >= depth else _modulate(x, sc_n, sh_n, CDT)
    return x
```

```python
import functools

import jax
import jax.numpy as jnp
import numpy as np
from jax import lax
from jax.experimental import pallas as pl
from jax.experimental.pallas import tpu as pltpu

F32 = jnp.float32
CDT = jnp.bfloat16

D = 2048
DEPTH = 2
H_MLA, NOPE, ROPE_MLA, V_MLA = 8, 128, 64, 128
Q_RANK, KV_RANK = 512, 256
H_DSA, G_DSA, HD_DSA = 8, 2, 128
ROT_DSA = HD_DSA // 4
H_IDX, D_IDX = 8, 64
ROT_IDX = D_IDX // 4
TOPK_MAX = 256
CONV_W, CONV_K = 1024, 3
THETA = 500000.0
N_EXPERTS, TOP_K = 8, 2
ALPHA = (2 * DEPTH) ** 0.25
LN_EPS = 1e-5
RMS_EPS = 1e-6
SPLITS = (Q_RANK, KV_RANK, ROPE_MLA, H_DSA * HD_DSA, G_DSA * HD_DSA, G_DSA * HD_DSA,
          H_IDX * D_IDX, D_IDX, H_IDX, CONV_W, CONV_W, CONV_W, 3 * D)

LANES = 128
NEG = -1e30
LOG2E = 1.4426950408889634
VMEM_LIMIT = 56 * 1024 * 1024

MOE_BLK = 512
MOE_GROUP = 3
MLA_BLK = 512


def _params(sem, vmem=VMEM_LIMIT):
    return pltpu.CompilerParams(dimension_semantics=sem, vmem_limit_bytes=vmem)


def _sigmoid(x):
    return 1.0 / (1.0 + jnp.exp(-x))


def _dot(a, b):
    return jnp.dot(a, b, preferred_element_type=F32)


def _dot_t(a, b):
    return lax.dot_general(a, b, (((1,), (1,)), ((), ())), preferred_element_type=F32)


def _layer_norm(y, g, b):
    mu = jnp.mean(y, -1, keepdims=True)
    d = y - mu
    var = jnp.mean(d * d, -1, keepdims=True)
    return d * lax.rsqrt(var + LN_EPS) * g + b


def _rope(y, c, s, half):
    lane = lax.broadcasted_iota(jnp.int32, y.shape, 1)
    swapped = jnp.where(lane < half, pltpu.roll(y, LANES - half, 1), pltpu.roll(y, half, 1))
    return y * c + swapped * s


def _ada_body(c_ref, w_ref, b_ref, o_ref):
    c = c_ref[...]
    ca = (c * _sigmoid(c)).astype(CDT)
    o_ref[...] = _dot(ca, w_ref[...].astype(CDT)) + b_ref[...]


def _ada(c, ada_w, ada_b):
    depth, d, n = ada_w.shape
    bsz = c.shape[0]
    rows = 8
    cp = jnp.zeros((rows, d), F32).at[:bsz].set(c)
    tn = 1024
    out = pl.pallas_call(
        _ada_body,
        grid=(depth, n // tn),
        in_specs=[pl.BlockSpec((rows, d), lambda l, j: (0, 0)),
                  pl.BlockSpec((None, d, tn), lambda l, j: (l, 0, j)),
                  pl.BlockSpec((None, 1, tn), lambda l, j: (l, 0, j))],
        out_specs=pl.BlockSpec((None, rows, tn), lambda l, j: (l, 0, j)),
        out_shape=jax.ShapeDtypeStruct((depth, rows, n), F32),
        compiler_params=_params(("arbitrary", "arbitrary")),
        name="ada_mod",
    )(cp, ada_w, ada_b.reshape(depth, 1, n))
    return out[:, :bsz]


def _mod_body(x_ref, sc_ref, sh_ref, o_ref):
    o_ref[...] = (x_ref[...] * (1.0 + sc_ref[...]) + sh_ref[...]).astype(o_ref.dtype)


def _modulate(x, sc, sh, out_dtype):
    bsz, s_len, d = x.shape
    tm = 512
    vec = pl.BlockSpec((None, 1, d), lambda b, i: (b, 0, 0))
    return pl.pallas_call(
        _mod_body,
        grid=(bsz, s_len // tm),
        in_specs=[pl.BlockSpec((None, tm, d), lambda b, i: (b, i, 0)), vec, vec],
        out_specs=pl.BlockSpec((None, tm, d), lambda b, i: (b, i, 0)),
        out_shape=jax.ShapeDtypeStruct(x.shape, out_dtype),
        compiler_params=_params(("arbitrary", "arbitrary")),
        name="modulate",
    )(x, sc, sh)


def _mla_prep_body(h_ref, w1_ref, qg_ref, kg_ref, wuq_ref, wukv_ref, c_ref, s_ref, q_ref, k_ref, v_ref):
    z = _dot(h_ref[...], w1_ref[...])
    cq = z[:, :Q_RANK]
    ckv = z[:, Q_RANK:Q_RANK + KV_RANK]
    kr = z[:, Q_RANK + KV_RANK:]
    nq = (cq * lax.rsqrt(jnp.mean(cq * cq, -1, keepdims=True) + RMS_EPS) * qg_ref[...]).astype(CDT)
    nkv = (ckv * lax.rsqrt(jnp.mean(ckv * ckv, -1, keepdims=True) + RMS_EPS) * kg_ref[...]).astype(CDT)
    qa = _dot(nq, wuq_ref[...])
    kva = _dot(nkv, wukv_ref[...])
    c = c_ref[...]
    s = s_ref[...]
    scale = (NOPE + ROPE_MLA) ** -0.5 * LOG2E
    krot_t = _rope(kr, c, s, ROPE_MLA // 2).T.astype(CDT)
    lane = lax.broadcasted_iota(jnp.int32, (h_ref.shape[0], LANES), 1)
    ones_col = jnp.where(lane == 0, 1.0, 0.0).astype(CDT)
    for h in range(H_MLA):
        lo = h * 2 * LANES
        q_ref[h, :, 0:LANES] = (qa[:, lo:lo + LANES] * scale).astype(CDT)
        q_ref[h, :, LANES:2 * LANES] = (_rope(qa[:, lo + LANES:lo + 2 * LANES], c, s, ROPE_MLA // 2)
                                        * scale).astype(CDT)
        k_ref[h, 0:LANES, :] = kva[:, lo:lo + LANES].T.astype(CDT)
        k_ref[h, LANES:2 * LANES, :] = krot_t
        v_ref[h, :, 0:LANES] = kva[:, lo + LANES:lo + 2 * LANES].astype(CDT)
        v_ref[h, :, LANES:2 * LANES] = ones_col


def _mla_prep(h, w1, qg, kg, wuq, wukv, ctab, stab):
    bsz, s_len, d = h.shape
    tm = min(MLA_BLK, s_len)
    full = lambda a: pl.BlockSpec(a.shape, lambda b, i: (0,) * a.ndim)
    row = lambda w: pl.BlockSpec((None, tm, w), lambda b, i: (b, i, 0))
    head = lambda w: pl.BlockSpec((None, H_MLA, tm, w), lambda b, i: (b, 0, i, 0))
    return pl.pallas_call(
        _mla_prep_body,
        grid=(bsz, s_len // tm),
        in_specs=[row(d), full(w1), full(qg), full(kg), full(wuq), full(wukv), row(LANES), row(LANES)],
        out_specs=[head(2 * LANES),
                   pl.BlockSpec((None, H_MLA, None, 2 * LANES, tm), lambda b, i: (b, 0, i, 0, 0)),
                   head(2 * LANES)],
        out_shape=[jax.ShapeDtypeStruct((bsz, H_MLA, s_len, 2 * LANES), CDT),
                   jax.ShapeDtypeStruct((bsz, H_MLA, s_len // tm, 2 * LANES, tm), CDT),
                   jax.ShapeDtypeStruct((bsz, H_MLA, s_len, 2 * LANES), CDT)],
        compiler_params=_params(("arbitrary", "arbitrary")),
        name="mla_prep",
    )(h, w1, qg, kg, wuq, wukv, ctab, stab)


def _flash_body(q_ref, k_ref, v_ref, o_ref, s_ref, mx_ref, m_ref, acc_ref, *, blk):
    qi = pl.program_id(2)
    dv = V_MLA

    def scores(hd, j, masked):
        s = _dot(q_ref[hd], k_ref[hd, j])
        if masked:
            row = lax.broadcasted_iota(jnp.int32, s.shape, 0)
            col = lax.broadcasted_iota(jnp.int32, s.shape, 1)
            s = jnp.where(col <= row, s, NEG)
        return s, jnp.max(s, -1, keepdims=True)

    def put(hd, sv):
        s_ref[hd], mx_ref[hd] = sv

    def accumulate(hd, j):
        off = pl.multiple_of(j * blk, blk)
        m_old = m_ref[hd]
        m_new = jnp.maximum(m_old, mx_ref[hd])
        alpha = jnp.exp2(m_old - m_new)
        p = jnp.exp2((s_ref[hd] - m_new).astype(CDT))
        acc_ref[hd] = alpha * acc_ref[hd] + _dot(p, v_ref[hd, pl.ds(off, blk), :])
        m_ref[hd] = m_new

    m_ref[...] = jnp.full(m_ref.shape, NEG, F32)
    acc_ref[...] = jnp.zeros(acc_ref.shape, F32)
    put(0, scores(0, qi, True))
    put(1, scores(1, qi, True))

    def body(n, _):
        j_prev = jnp.where(n == 1, qi, n - 2)
        accumulate(0, j_prev)
        accumulate(1, j_prev)
        put(0, scores(0, n - 1, False))
        put(1, scores(1, n - 1, False))
        return 0

    lax.fori_loop(1, qi + 1, body, 0)
    j_last = jnp.where(qi == 0, qi, qi - 1)
    accumulate(0, j_last)
    accumulate(1, j_last)
    for hd in range(2):
        acc = acc_ref[hd]
        o_ref[:, hd * dv:(hd + 1) * dv] = (acc[:, :dv] / acc[:, dv:dv + 1]).astype(o_ref.dtype)


def _mla_attention(q, k, v):
    bsz, nh, s_len, dq = q.shape
    dve = v.shape[-1]
    dv = V_MLA
    blk = min(MLA_BLK, s_len)
    return pl.pallas_call(
        functools.partial(_flash_body, blk=blk),
        grid=(bsz, nh // 2, s_len // blk),
        in_specs=[pl.BlockSpec((None, 2, blk, dq), lambda b, h, i: (b, h, i, 0)),
                  pl.BlockSpec((None, 2, s_len // blk, dq, blk), lambda b, h, i: (b, h, 0, 0, 0)),
                  pl.BlockSpec((None, 2, s_len, dve), lambda b, h, i: (b, h, 0, 0))],
        out_specs=pl.BlockSpec((None, blk, 2 * dv), lambda b, h, i: (b, i, h)),
        out_shape=jax.ShapeDtypeStruct((bsz, s_len, nh * dv), CDT),
        scratch_shapes=[pltpu.VMEM((2, blk, blk), F32), pltpu.VMEM((2, blk, 1), F32),
                        pltpu.VMEM((2, blk, 1), F32), pltpu.VMEM((2, blk, dve), F32)],
        compiler_params=_params(("arbitrary", "arbitrary", "arbitrary")),
        name="mla_attention",
    )(q, k, v)


DSA_CHUNK = 512
DSA_TQ = 128
VT_PAD = 16
VT_ROWS = HD_DSA + VT_PAD


def _dsa_prep_body(h_ref, w2_ref, cd_ref, sd_ref, ci_ref, si_ref, q_ref, k_ref, vt_ref, qi_ref, ki_ref, wit_ref):
    z = _dot(h_ref[...], w2_ref[...])
    cd, sd, ci, si = cd_ref[...], sd_ref[...], ci_ref[...], si_ref[...]
    col = lambda j: z[:, j * LANES:(j + 1) * LANES]
    for h in range(H_DSA):
        q_ref[h] = (_rope(col(h), cd, sd, ROT_DSA // 2) * (HD_DSA ** -0.5 * LOG2E)).T.astype(CDT)
    for g in range(G_DSA):
        k_ref[:, g * LANES:(g + 1) * LANES] = _rope(col(H_DSA + g), cd, sd, ROT_DSA // 2).astype(CDT)
    base = H_DSA + G_DSA
    sub = lax.broadcasted_iota(jnp.int32, (VT_PAD, z.shape[0]), 0)
    ones_row = jnp.where(sub == 0, 1.0, 0.0).astype(CDT)
    for g in range(G_DSA):
        vt_ref[g * VT_ROWS:g * VT_ROWS + LANES, :] = col(base + g).T.astype(CDT)
        vt_ref[g * VT_ROWS + LANES:(g + 1) * VT_ROWS, :] = ones_row
    base += G_DSA
    for h in range(H_IDX):
        qi_ref[h] = (_rope(col(base + h), ci, si, ROT_IDX // 2) * D_IDX ** -0.5).T.astype(CDT)
    base += H_IDX
    ki_ref[...] = _rope(col(base), ci, si, ROT_IDX // 2).astype(CDT)
    wit_ref[...] = col(base + 1).T[0:H_IDX, :] * H_IDX ** -0.5


def _dsa_prep(h, w2, cd, sd, ci, si):
    bsz, s_len, d = h.shape
    tm = min(DSA_CHUNK, s_len)
    row = lambda w: pl.BlockSpec((None, tm, w), lambda b, i: (b, i, 0))
    return pl.pallas_call(
        _dsa_prep_body,
        grid=(bsz, s_len // tm),
        in_specs=[row(d), pl.BlockSpec(w2.shape, lambda b, i: (0, 0)), row(LANES), row(LANES), row(LANES),
                  row(LANES)],
        out_specs=[pl.BlockSpec((None, H_DSA, LANES, tm), lambda b, i: (b, 0, 0, i)), row(G_DSA * LANES),
                   pl.BlockSpec((None, None, G_DSA * VT_ROWS, tm), lambda b, i: (b, i, 0, 0)),
                   pl.BlockSpec((None, H_IDX, LANES, tm), lambda b, i: (b, 0, 0, i)), row(LANES),
                   pl.BlockSpec((None, H_IDX, tm), lambda b, i: (b, 0, i))],
        out_shape=[jax.ShapeDtypeStruct((bsz, H_DSA, LANES, s_len), CDT),
                   jax.ShapeDtypeStruct((bsz, s_len, G_DSA * LANES), CDT),
                   jax.ShapeDtypeStruct((bsz, s_len // tm, G_DSA * VT_ROWS, tm), CDT),
                   jax.ShapeDtypeStruct((bsz, H_IDX, LANES, s_len), CDT),
                   jax.ShapeDtypeStruct((bsz, s_len, LANES), CDT),
                   jax.ShapeDtypeStruct((bsz, H_IDX, s_len), F32)],
        compiler_params=_params(("arbitrary", "arbitrary")),
        name="dsa_prep",
    )(h, w2, cd, sd, ci, si)


FOLD_ROWS = 64


def _fold(x):
    return jnp.sum(x.reshape(x.shape[0] // FOLD_ROWS, FOLD_ROWS, x.shape[1]), axis=0)


def _dsa_body(qi_ref, ki_ref, wit_ref, q_ref, k_ref, vt_ref, tri_ref, o_ref,
              key_ref, sel_ref, qia_ref, qg_ref, s_ref, mx_ref, m_ref, acc_ref, *, tq, tkc, n_sel):
    i = pl.program_id(1)
    q0 = i * tq
    nkc = (q0 + tq + tkc - 1) // tkc
    n_rep = H_DSA // G_DSA
    key_neg_inf = jnp.int32(-8388608) ^ jnp.int32(0x7FFFFFFF)
    qpos = q0 + lax.broadcasted_iota(jnp.int32, (tkc, tq), 1)

    for h in range(H_IDX):
        qia_ref[:, h * tq:(h + 1) * tq] = qi_ref[h]
    for g in range(G_DSA):
        for r in range(n_rep):
            qg_ref[g, :, r * tq:(r + 1) * tq] = q_ref[g * n_rep + r]

    def score_chunk(c, _):
        off = pl.multiple_of(c * tkc, tkc)
        kc = ki_ref[pl.ds(off, tkc), :]
        w = wit_ref[...]
        lg = _dot(kc, qia_ref[...])
        sc = jnp.zeros((tkc, tq), F32)
        for h in range(H_IDX):
            sc = sc + w[h:h + 1, :] * jnp.maximum(lg[:, h * tq:(h + 1) * tq], 0.0)
        kpos = off + lax.broadcasted_iota(jnp.int32, (tkc, tq), 0)
        sc = jnp.where(kpos <= qpos, sc, -jnp.inf)
        bits = pltpu.bitcast(sc, jnp.int32)
        key_ref[c] = jnp.where(bits >= 0, bits, bits ^ jnp.int32(0x7FFFFFFF))
        return 0

    lax.fori_loop(0, nkc, score_chunk, 0)

    def count(pred):
        def body(c, cnt):
            return cnt + _fold(jnp.where(pred(key_ref[c]), 1.0, 0.0))
        part = lax.fori_loop(0, nkc, body, jnp.zeros((FOLD_ROWS, tq), F32))
        return jnp.sum(part, axis=0, keepdims=True)

    def bit_step(bi, thr):
        cand = thr + lax.shift_left(jnp.int32(1), 31 - bi)
        cnt = count(lambda kk: kk >= cand)
        return jnp.where(cnt >= n_sel, cand, thr)

    thr = lax.fori_loop(0, 32, bit_step, jnp.full((1, tq), jnp.iinfo(jnp.int32).min, jnp.int32))

    cnt_gt = count(lambda kk: kk > thr)
    cnt_eq = count(lambda kk: kk == thr)
    need = n_sel - cnt_gt
    excess = jnp.where(thr > key_neg_inf, cnt_eq - need, 0.0)

    def mask_plain():
        def body(c, _):
            kk = key_ref[c]
            sel_ref[c] = jnp.where(kk >= thr, jnp.where(kk > key_neg_inf, 0.0, NEG), NEG)
            return 0
        lax.fori_loop(0, nkc, body, 0)

    def mask_ranked():
        def body(c, seen):
            kk = key_ref[c]
            tie = kk == thr
            pre = _dot(tri_ref[...], jnp.where(tie, 1.0, 0.0).astype(CDT))
            rank = seen + pre
            keep_tie = jnp.where(tie, jnp.where(rank <= need, 0.0, NEG), NEG)
            keep = jnp.where(kk > thr, 0.0, keep_tie)
            sel_ref[c] = jnp.where(kk > key_neg_inf, keep, NEG)
            return seen + pre[tkc - 1:tkc, :]
        lax.fori_loop(0, nkc, body, jnp.zeros((1, tq), F32))

    lax.cond(jnp.max(excess) > 0.0, mask_ranked, mask_plain)

    m_ref[...] = jnp.full(m_ref.shape, NEG, F32)
    acc_ref[...] = jnp.zeros(acc_ref.shape, F32)

    def scores(g, c):
        off = pl.multiple_of(c * tkc, tkc)
        msk = sel_ref[c]
        msk = jnp.concatenate([msk] * n_rep, axis=1)
        kg = k_ref[pl.ds(off, tkc), g * LANES:(g + 1) * LANES]
        s = _dot(kg, qg_ref[g]) + msk
        return s, jnp.max(s, axis=0, keepdims=True)

    def put(g, sv):
        s_ref[g], mx_ref[g] = sv

    def accumulate(g, c):
        m_old = m_ref[g]
        m_new = jnp.maximum(m_old, mx_ref[g])
        alpha = jnp.exp2(m_old - m_new)
        p = jnp.exp2((s_ref[g] - m_new).astype(CDT))
        acc_ref[g] = alpha * acc_ref[g] + _dot(vt_ref[c, g * VT_ROWS:(g + 1) * VT_ROWS, :], p)
        m_ref[g] = m_new

    put(0, scores(0, 0))
    put(1, scores(1, 0))

    def att_chunk(c, _):
        sv0 = scores(0, c)
        sv1 = scores(1, c)
        accumulate(0, c - 1)
        accumulate(1, c - 1)
        put(0, sv0)
        put(1, sv1)
        return 0

    lax.fori_loop(1, nkc, att_chunk, 0)
    accumulate(0, nkc - 1)
    accumulate(1, nkc - 1)
    for g in range(G_DSA):
        acc = acc_ref[g]
        out = acc[0:HD_DSA, :] / acc[HD_DSA:HD_DSA + 1, :]
        for r in range(n_rep):
            hh = g * n_rep + r
            o_ref[:, hh * LANES:(hh + 1) * LANES] = out[:, r * tq:(r + 1) * tq].T.astype(o_ref.dtype)


def _dsa_attention(qi, ki, wit, q, k, vt):
    bsz, s_len, _ = k.shape
    tq = min(DSA_TQ, s_len)
    tkc = min(DSA_CHUNK, s_len)
    n_chunks = s_len // tkc
    n_sel = min(TOPK_MAX, s_len // 4)
    n_rep = H_DSA // G_DSA
    tri =(lax.broadcasted_iota(jnp.int32, (tkc, tkc), 1)
           <= lax.broadcasted_iota(jnp.int32, (tkc, tkc), 0)).astype(CDT)
    row = lambda w: pl.BlockSpec((None, tq, w), lambda b, i: (b, i, 0))
    seq = lambda w: pl.BlockSpec((None, s_len, w), lambda b, i: (b, 0, 0))
    return pl.pallas_call(
        functools.partial(_dsa_body, tq=tq, tkc=tkc, n_sel=n_sel),
        grid=(bsz, s_len // tq),
        in_specs=[pl.BlockSpec((None, H_IDX, LANES, tq), lambda b, i: (b, 0, 0, i)), seq(LANES),
                  pl.BlockSpec((None, H_IDX, tq), lambda b, i: (b, 0, i)),
                  pl.BlockSpec((None, H_DSA, LANES, tq), lambda b, i: (b, 0, 0, i)), seq(G_DSA * LANES),
                  pl.BlockSpec((None, n_chunks, G_DSA * VT_ROWS, tkc), lambda b, i: (b, 0, 0, 0)),
                  pl.BlockSpec((tkc, tkc), lambda b, i: (0, 0))],
        out_specs=row(H_DSA * LANES),
        out_shape=jax.ShapeDtypeStruct((bsz, s_len, H_DSA * LANES), CDT),
        scratch_shapes=[pltpu.VMEM((n_chunks, tkc, tq), jnp.int32),
                        pltpu.VMEM((n_chunks, tkc, tq), F32),
                        pltpu.VMEM((LANES, H_IDX * tq), CDT),
                        pltpu.VMEM((G_DSA, LANES, n_rep * tq), CDT),
                        pltpu.VMEM((G_DSA, tkc, n_rep * tq), F32),
                        pltpu.VMEM((G_DSA, 1, n_rep * tq), F32),
                        pltpu.VMEM((G_DSA, 1, n_rep * tq), F32),
                        pltpu.VMEM((G_DSA, VT_ROWS, n_rep * tq), F32)],
        compiler_params=_params(("arbitrary", "arbitrary")),
        name="dsa_attention",
    )(qi, ki, wit, q, k, vt, tri)


def _conv_body(h_ref, w_ref, cw_ref, cb_ref, o_ref, tail_ref, *, tm, tc):
    @pl.when(pl.program_id(2) == 0)
    def _():
        tail_ref[...] = jnp.zeros(tail_ref.shape, F32)

    z = _dot(h_ref[...], w_ref[...])
    gb = z[:, :tc]
    u = z[:, tc:2 * tc] * z[:, 2 * tc:]
    prev = tail_ref[...]
    row = lax.broadcasted_iota(jnp.int32, (tm, tc), 0)
    u1 = jnp.where(row == 0, prev[7:8, :], pltpu.roll(u, 1, 0))
    u2 = jnp.where(row == 0, prev[6:7, :], jnp.where(row == 1, prev[7:8, :], pltpu.roll(u, 2, 0)))
    cw = cw_ref[...]
    y = cw[0:1, :] * u2 + cw[1:2, :] * u1 + cw[2:3, :] * u + cb_ref[...]
    o_ref[...] = (gb * y).astype(o_ref.dtype)
    tail_ref[...] = u[tm - 8:tm, :]


def _conv_branch(h, w3, conv_w, conv_b, tc):
    bsz, s_len, d = h.shape
    tm = 512
    n_cb = CONV_W // tc
    return pl.pallas_call(
        functools.partial(_conv_body, tm=tm, tc=tc),
        grid=(n_cb, bsz, s_len // tm),
        in_specs=[pl.BlockSpec((None, tm, d), lambda cb, b, i: (b, i, 0)),
                  pl.BlockSpec((d, 3 * tc), lambda cb, b, i: (0, cb)),
                  pl.BlockSpec((CONV_K, tc), lambda cb, b, i: (0, cb)),
                  pl.BlockSpec((1, tc), lambda cb, b, i: (0, cb))],
        out_specs=pl.BlockSpec((None, tm, tc), lambda cb, b, i: (b, i, cb)),
        out_shape=jax.ShapeDtypeStruct((bsz, s_len, CONV_W), CDT),
        scratch_shapes=[pltpu.VMEM((8, tc), F32)],
        compiler_params=_params(("arbitrary", "arbitrary", "arbitrary")),
        name="conv_branch",
    )(h, w3, conv_w, conv_b.reshape(1, CONV_W))


def _merge_body(h_ref, ya_ref, yb_ref, yc_ref, wg_ref, wbr_ref, o_ref):
    h = h_ref[...]
    acc = None
    for j, y_ref in enumerate((ya_ref, yb_ref, yc_ref)):
        term = _sigmoid(_dot(h, wg_ref[j])) * _dot(y_ref[...], wbr_ref[j])
        acc = term if acc is None else acc + term
    o_ref[...] = acc.astype(o_ref.dtype)


def _merge(h, ya, yb, yc, wg, wbr):
    t, d = h.shape
    tm, tn = 512, 512
    yw = ya.shape[1]
    row = lambda w: pl.BlockSpec((tm, w), lambda j, i: (i, 0))
    return pl.pallas_call(
        _merge_body,
        grid=(d // tn, t // tm),
        in_specs=[row(d), row(yw), row(yw), row(yw),
                  pl.BlockSpec((3, d, tn), lambda j, i: (0, 0, j)),
                  pl.BlockSpec((3, yw, tn), lambda j, i: (0, 0, j))],
        out_specs=pl.BlockSpec((tm, tn), lambda j, i: (i, j)),
        out_shape=jax.ShapeDtypeStruct((t, d), CDT),
        compiler_params=_params(("arbitrary", "arbitrary")),
        name="gated_merge",
    )(h, ya, yb, yc, wg, wbr)


def _out_ln_body(m_ref, wo_ref, x_ref, gm_ref, lg_ref, lb_ref, sc_ref, sh_ref, x1_ref, h_ref):
    y = ALPHA * x_ref[...] + gm_ref[...] * _dot(m_ref[...], wo_ref[...])
    x1 = _layer_norm(y, lg_ref[...], lb_ref[...])
    x1_ref[...] = x1
    h_ref[...] = (x1 * (1.0 + sc_ref[...]) + sh_ref[...]).astype(h_ref.dtype)


def _out_ln(merged, wo, x, gm, lg, lb, sc, sh, h_dtype):
    bsz, s_len, d = x.shape
    tm = 256
    row = pl.BlockSpec((None, tm, d), lambda b, i: (b, i, 0))
    per_b = pl.BlockSpec((None, 1, d), lambda b, i: (b, 0, 0))
    vec = pl.BlockSpec((1, d), lambda b, i: (0, 0))
    return pl.pallas_call(
        _out_ln_body,
        grid=(bsz, s_len // tm),
        in_specs=[row, pl.BlockSpec((d, d), lambda b, i: (0, 0)), row, per_b, vec, vec, per_b, per_b],
        out_specs=[row, row],
        out_shape=[jax.ShapeDtypeStruct(x.shape, F32), jax.ShapeDtypeStruct(x.shape, h_dtype)],
        compiler_params=_params(("arbitrary", "arbitrary")),
        name="out_proj_ln",
    )(merged, wo, x, gm, lg, lb, sc, sh)


def _ffn_body(h_ref, wg_ref, wu_ref, wd_ref, x_ref, gf_ref, lg_ref, lb_ref, sc_ref, sh_ref, x2_ref, h2_ref,
              acc_ref):
    k = pl.program_id(2)

    @pl.when(k == 0)
    def _():
        acc_ref[...] = jnp.zeros(acc_ref.shape, F32)

    h = h_ref[...]
    a = _dot(h, wg_ref[...])
    act = (a * _sigmoid(a) * _dot(h, wu_ref[...])).astype(CDT)
    acc_ref[...] += _dot(act, wd_ref[...])

    @pl.when(k == pl.num_programs(2) - 1)
    def _():
        y = ALPHA * x_ref[...] + gf_ref[...] * acc_ref[...]
        x2 = _layer_norm(y, lg_ref[...], lb_ref[...])
        x2_ref[...] = x2
        h2_ref[...] = (x2 * (1.0 + sc_ref[...]) + sh_ref[...]).astype(h2_ref.dtype)


def _ffn_ln(h, wg, wu, wd, x, gf, lg, lb, sc, sh):
    bsz, s_len, d = x.shape
    f = wg.shape[1]
    tm, tf = 512, 512
    row = pl.BlockSpec((None, tm, d), lambda b, i, k: (b, i, 0))
    per_b = pl.BlockSpec((None, 1, d), lambda b, i, k: (b, 0, 0))
    vec = pl.BlockSpec((1, d), lambda b, i, k: (0, 0))
    return pl.pallas_call(
        _ffn_body,
        grid=(bsz, s_len // tm, f // tf),
        in_specs=[row, pl.BlockSpec((d, tf), lambda b, i, k: (0, k)), pl.BlockSpec((d, tf), lambda b, i, k: (0, k)),
                  pl.BlockSpec((tf, d), lambda b, i, k: (k, 0)), row, per_b, vec, vec, per_b, per_b],
        out_specs=[row, row],
        out_shape=[jax.ShapeDtypeStruct(x.shape, F32), jax.ShapeDtypeStruct(x.shape, CDT)],
        scratch_shapes=[pltpu.VMEM((tm, d), F32)],
        compiler_params=_params(("arbitrary", "arbitrary", "arbitrary")),
        name="ffn_ln",
    )(h, wg, wu, wd, x, gf, lg, lb, sc, sh)


def _router_body(h_ref, wr_ref, e_ref, g_ref):
    lg = jnp.dot(h_ref[...], wr_ref[...], preferred_element_type=F32, precision=lax.Precision.HIGHEST)
    lane = lax.broadcasted_iota(jnp.int32, lg.shape, 1)
    lanef = lane.astype(F32)
    lg = jnp.where(lane < N_EXPERTS, lg, -jnp.inf)
    m1 = jnp.max(lg, -1, keepdims=True)
    i1 = jnp.min(jnp.where(lg == m1, lanef, float(LANES)), -1, keepdims=True)
    lg2 = jnp.where(lanef == i1, -jnp.inf, lg)
    m2 = jnp.max(lg2, -1, keepdims=True)
    i2 = jnp.min(jnp.where(lg2 == m2, lanef, float(LANES)), -1, keepdims=True)
    e = jnp.exp(m2 - m1)
    g1 = 1.0 / (1.0 + e)
    g2 = e / (1.0 + e)
    e_ref[...] = jnp.where(lane == 0, i1, jnp.where(lane == 1, i2, 0.0)).astype(jnp.int32)
    g_ref[...] = jnp.where(lane == 0, g1, jnp.where(lane == 1, g2, 0.0))


def _router(h, wr):
    t, d = h.shape
    tm = 512
    return pl.pallas_call(
        _router_body,
        grid=(t // tm,),
        in_specs=[pl.BlockSpec((tm, d), lambda i: (i, 0)), pl.BlockSpec((d, LANES), lambda i: (0, 0))],
        out_specs=[pl.BlockSpec((tm, LANES), lambda i: (i, 0)), pl.BlockSpec((tm, LANES), lambda i: (i, 0))],
        out_shape=[jax.ShapeDtypeStruct((t, LANES), jnp.int32), jax.ShapeDtypeStruct((t, LANES), F32)],
        compiler_params=_params(("arbitrary",)),
        name="moe_router",
    )(h, wr)


def _moe_body(tok_ref, ge_ref, bv_ref, h_hbm, wg_ref, wu_ref, wd_ref, y_hbm, xbuf, xs, y_ref, sem, out_sem, *, blk):
    i = pl.program_id(0)
    k = pl.program_id(1)
    ng = pl.num_programs(0)
    nk = pl.num_programs(1)
    rows = MOE_GROUP * blk

    def out_copy(gi):
        return pltpu.make_async_copy(y_ref, y_hbm.at[pl.ds(pl.multiple_of(gi * rows, rows), rows), :], out_sem)

    def issue(gi):
        def body(r, _):
            pltpu.make_async_copy(h_hbm.at[pl.ds(tok_ref[gi * rows + r], 1), :], xbuf.at[pl.ds(r, 1), :], sem).start()
            return 0
        lax.fori_loop(0, rows, body, 0)

    @pl.when(k == 0)
    def _():
        @pl.when(i == 0)
        def _():
            issue(0)

        pltpu.make_async_copy(h_hbm.at[pl.ds(0, rows), :], xbuf, sem).wait()
        xs[...] = xbuf[...].astype(CDT)

        @pl.when(i + 1 < ng)
        def _():
            issue(i + 1)

        @pl.when(i > 0)
        def _():
            out_copy(i - 1).wait()

        y_ref[...] = jnp.zeros(y_ref.shape, F32)

    def ffn(n_blocks):
        wg = wg_ref[...].astype(CDT)
        wu = wu_ref[...].astype(CDT)
        wd = wd_ref[...].astype(CDT)
        for b in range(n_blocks):
            x = xs[b * blk:(b + 1) * blk, :]
            a = _dot(x, wg)
            act = (a * _sigmoid(a) * _dot(x, wu)).astype(CDT)
            y_ref[b * blk:(b + 1) * blk, :] += _dot(act, wd)

    live = bv_ref[i * MOE_GROUP]
    for b in range(1, MOE_GROUP):
        live = live + bv_ref[i * MOE_GROUP + b]
    for n_blocks in range(1, MOE_GROUP + 1):
        pl.when(live == n_blocks)(functools.partial(ffn, n_blocks))

    @pl.when(k == nk - 1)
    def _():
        out_copy(i).start()

        @pl.when(i == ng - 1)
        def _():
            out_copy(i).wait()


def _moe_ffn(h, row_tok, grp_e, blk_valid, wg, wu, wd):
    t, d = h.shape
    n_rows = row_tok.shape[0]
    blk = MOE_BLK
    rows = MOE_GROUP * blk
    ng = n_rows // rows
    f = wg.shape[2]
    tf = 256
    nk = f // tf
    kk = lambda k, i, bv: jnp.where(bv[i * MOE_GROUP] == 1, k, nk - 1)
    grid_spec = pltpu.PrefetchScalarGridSpec(
        num_scalar_prefetch=3,
        grid=(ng, nk),
        in_specs=[pl.BlockSpec(memory_space=pl.ANY),
                  pl.BlockSpec((None, d, tf), lambda i, k, tok, ge, bv: (ge[i], 0, kk(k, i, bv))),
                  pl.BlockSpec((None, d, tf), lambda i, k, tok, ge, bv: (ge[i], 0, kk(k, i, bv))),
                  pl.BlockSpec((None, tf, d), lambda i, k, tok, ge, bv: (ge[i], kk(k, i, bv), 0))],
        out_specs=pl.BlockSpec(memory_space=pl.ANY),
        scratch_shapes=[pltpu.VMEM((rows, d), F32), pltpu.VMEM((rows, d), CDT), pltpu.VMEM((rows, d), F32),
                        pltpu.SemaphoreType.DMA(()), pltpu.SemaphoreType.DMA(())],
    )
    return pl.pallas_call(
        functools.partial(_moe_body, blk=blk),
        grid_spec=grid_spec,
        out_shape=jax.ShapeDtypeStruct((n_rows, d), F32),
        compiler_params=_params(("arbitrary", "arbitrary")),
        name="moe_grouped_ffn",
    )(row_tok, grp_e, blk_valid, h, wg, wu, wd)


def _combine_body(pos_ref, y_hbm, gate_ref, x_ref, gf_ref, lg_ref, lb_ref, o_ref, ybuf, sem, *, tm):
    i = pl.program_id(0)
    nb = pl.num_programs(0)

    def issue(bi, slot):
        def body(r, _):
            for s in range(TOP_K):
                p = pos_ref[(bi * tm + r) * TOP_K + s]
                pltpu.make_async_copy(y_hbm.at[pl.ds(p, 1), :], ybuf.at[slot, s, pl.ds(r, 1), :],
                                      sem.at[slot]).start()
            return 0
        lax.fori_loop(0, tm, body, 0)

    @pl.when(i == 0)
    def _():
        issue(0, 0)

    slot = i % 2
    for s in range(TOP_K):
        pltpu.make_async_copy(y_hbm.at[pl.ds(0, tm), :], ybuf.at[slot, s], sem.at[slot]).wait()

    @pl.when(i + 1 < nb)
    def _():
        issue(i + 1, (i + 1) % 2)

    gate = gate_ref[...]
    f = ybuf[slot, 0] * gate[:, 0:1] + ybuf[slot, 1] * gate[:, 1:2]
    y = ALPHA * x_ref[...] + gf_ref[...] * f
    o_ref[...] = _layer_norm(y, lg_ref[...], lb_ref[...])


def _combine_ln(y_rows, pos, gate, x, gf, lg, lb):
    bsz, s_len, d = x.shape
    tm = 256
    nsb = s_len // tm
    grid_spec = pltpu.PrefetchScalarGridSpec(
        num_scalar_prefetch=1,
        grid=(bsz * nsb,),
        in_specs=[pl.BlockSpec(memory_space=pl.ANY),
                  pl.BlockSpec((tm, LANES), lambda i, pos: (i, 0)),
                  pl.BlockSpec((None, tm, d), lambda i, pos: (i // nsb, i % nsb, 0)),
                  pl.BlockSpec((None, 1, d), lambda i, pos: (i // nsb, 0, 0)),
                  pl.BlockSpec((1, d), lambda i, pos: (0, 0)),
                  pl.BlockSpec((1, d), lambda i, pos: (0, 0))],
        out_specs=pl.BlockSpec((None, tm, d), lambda i, pos: (i // nsb, i % nsb, 0)),
        scratch_shapes=[pltpu.VMEM((2, TOP_K, tm, d), F32), pltpu.SemaphoreType.DMA((2,))],
    )
    return pl.pallas_call(
        functools.partial(_combine_body, tm=tm),
        grid_spec=grid_spec,
        out_shape=jax.ShapeDtypeStruct(x.shape, F32),
        compiler_params=_params(("arbitrary",)),
        name="moe_combine_ln",
    )(pos, y_rows, gate, x, gf, lg, lb)


def _routing_tables(top_e, n_tok):
    blk = MOE_BLK
    rows = MOE_GROUP * blk
    flat_e = top_e.reshape(-1)
    onehot = (flat_e[:, None] == jnp.arange(N_EXPERTS, dtype=jnp.int32)[None, :]).astype(jnp.int32)
    csum = jnp.cumsum(onehot, axis=0)
    counts = csum[-1]
    rank = jnp.sum(csum * onehot, axis=1) - 1
    padded = (counts + rows - 1) // rows * rows
    pend = jnp.cumsum(padded)
    pstart = pend - padded
    pos = (jnp.sum(pstart[None, :] * onehot, axis=1) + rank).astype(jnp.int32)
    n_rows = ((n_tok * TOP_K + N_EXPERTS * (rows - 1)) // rows + 1) * rows
    flat_tok = jnp.repeat(jnp.arange(n_tok, dtype=jnp.int32), TOP_K)
    row_tok = jnp.zeros((n_rows,), jnp.int32).at[pos].set(flat_tok)
    grp_start = jnp.arange(n_rows // rows, dtype=jnp.int32) * rows
    last_start = jnp.maximum(pend[-1] - rows, 0)
    grp_e = jnp.sum((pend[None, :] <= jnp.minimum(grp_start, last_start)[:, None]).astype(jnp.int32), axis=1)
    grp_e = jnp.minimum(grp_e, N_EXPERTS - 1)
    blk_start = jnp.arange(n_rows // blk, dtype=jnp.int32) * blk
    blk_e = jnp.repeat(grp_e, MOE_GROUP)
    blk_valid = ((blk_start < pend[-1]) & (blk_start - pstart[blk_e] < counts[blk_e])).astype(jnp.int32)
    return pos, row_tok, grp_e, blk_valid


def _rope_tables(positions, rot_dim, pad_value):
    half = rot_dim // 2
    inv = jnp.asarray(THETA ** (-np.arange(0, rot_dim, 2, dtype=np.float32) / np.float32(rot_dim)), F32)
    ang = positions.astype(F32)[..., None] * inv
    cos, sin = jnp.cos(ang), jnp.sin(ang)
    rest = positions.shape + (LANES - 2 * half,)
    c = jnp.concatenate([cos, cos, jnp.full(rest, pad_value, F32)], -1)
    s = jnp.concatenate([-sin, sin, jnp.zeros(rest, F32)], -1)
    return c, s


def _pad_cols(w, width):
    return jnp.pad(w, ((0, 0), (0, width - w.shape[1])))


def _mixer_weights(w_in, w_uq, w_ukv, w_a, w_b, w_c, tc):
    offs = [0]
    for n in SPLITS:
        offs.append(offs[-1] + n)
    part = lambda j: w_in[:, offs[j]:offs[j + 1]]
    d = w_in.shape[0]
    w1 = _pad_cols(jnp.concatenate([part(0), part(1), part(2)], 1), Q_RANK + KV_RANK + LANES)
    qi = jnp.pad(part(6).reshape(d, H_IDX, D_IDX), ((0, 0), (0, 0), (0, LANES - D_IDX))).reshape(d, H_IDX * LANES)
    w2 = jnp.concatenate([part(3), part(4), part(5), qi, _pad_cols(part(7), LANES), _pad_cols(part(8), LANES)], 1)
    n_cb = CONV_W // tc
    w3 = jnp.stack([part(9).reshape(d, n_cb, tc), part(10).reshape(d, n_cb, tc), part(11).reshape(d, n_cb, tc)],
                   axis=2).reshape(d, 3 * CONV_W)
    wg = part(12).reshape(d, 3, d).transpose(1, 0, 2)
    wuq = jnp.pad(w_uq.reshape(Q_RANK, H_MLA, NOPE + ROPE_MLA),
                  ((0, 0), (0, 0), (0, 2 * LANES - NOPE - ROPE_MLA))).reshape(Q_RANK, H_MLA * 2 * LANES)
    wbr = jnp.stack([w_a, w_b, w_c], 0)
    cast = lambda a: a.astype(CDT)
    return cast(w1), cast(w2), cast(w3), cast(wg), cast(wuq), cast(w_ukv), cast(wbr)


def kernel(x, c, positions, ada_w, ada_b, ln1_g, ln1_b, ln2_g, ln2_b, w_in, mla_q_norm, mla_kv_norm, w_uq, w_ukv,
           conv_w, conv_b, w_branch_a, w_branch_b, w_branch_c, w_o, ffn_w_gate, ffn_w_up, ffn_w_down, router_w,
           moe_w_gate, moe_w_up, moe_w_down):
    bsz, s_len, d = x.shape
    n_tok = bsz * s_len
    depth = ada_w.shape[0]
    conv_tc = 512

    cm, sm = _rope_tables(positions, ROPE_MLA, 0.0)
    cd, sd = _rope_tables(positions, ROT_DSA, 1.0)
    ci, si = _rope_tables(positions, ROT_IDX, 1.0)

    mod = _ada(c, ada_w, ada_b)
    vec = lambda a: a.reshape(1, d)

    h = None
    for i in range(depth):
        sh_m, sc_m, g_m, sh_f, sc_f, g_f = [mod[i, :, j * d:(j + 1) * d].reshape(bsz, 1, d) for j in range(6)]
        if h is None:
            h = _modulate(x, sc_m, sh_m, CDT)
        w1, w2, w3, wg, wuq, wukv, wbr = _mixer_weights(w_in[i], w_uq[i], w_ukv[i], w_branch_a[i], w_branch_b[i],
                                                       w_branch_c[i], conv_tc)

        qa, ka, va = _mla_prep(h, w1, mla_q_norm[i].reshape(1, -1), mla_kv_norm[i].reshape(1, -1), wuq, wukv, cm, sm)
        ya = _mla_attention(qa, ka, va)

        qd, kd, vd, qi, ki, wi = _dsa_prep(h, w2, cd, sd, ci, si)
        yb = _dsa_attention(qi, ki, wi, qd, kd, vd)

        yc = _conv_branch(h, w3, conv_w[i], conv_b[i], conv_tc)

        merged = _merge(h.reshape(n_tok, d), ya.reshape(n_tok, -1), yb.reshape(n_tok, -1), yc.reshape(n_tok, -1),
                        wg, wbr)
        moe_layer = i % 2 == 1
        x, h = _out_ln(merged.reshape(bsz, s_len, d), w_o[i].astype(CDT), x, g_m, vec(ln1_g[i]), vec(ln1_b[i]),
                       sc_f, sh_f, F32 if moe_layer else CDT)

        j = i // 2
        if i + 1 < depth:
            sh_n, sc_n = [mod[i + 1, :, q * d:(q + 1) * d].reshape(bsz, 1, d) for q in range(2)]
        else:
            sh_n, sc_n = jnp.zeros((bsz, 1, d), F32), jnp.zeros((bsz, 1, d), F32)
        if not moe_layer:
            x, h = _ffn_ln(h, ffn_w_gate[j].astype(CDT), ffn_w_up[j].astype(CDT), ffn_w_down[j].astype(CDT), x, g_f,
                           vec(ln2_g[i]), vec(ln2_b[i]), sc_n, sh_n)
        else:
            ht = h.reshape(n_tok, d)
            top_e, top_g = _router(ht, _pad_cols(router_w[j], LANES))
            pos, row_tok, grp_e, blk_valid = _routing_tables(top_e[:, :TOP_K], n_tok)
            y_rows = _moe_ffn(ht, row_tok, grp_e, blk_valid, moe_w_gate[j], moe_w_up[j], moe_w_down[j])
            x = _combine_ln(y_rows, pos, top_g, x, g_f, vec(ln2_g[i]), vec(ln2_b[i]))
            h = None if i + 1 >= depth else _modulate(x, sc_n, sh_n, CDT)
    return x
```

```python
import functools

import jax
import jax.numpy as jnp
import numpy as np
from jax import lax
from jax.experimental import pallas as pl
from jax.experimental.pallas import tpu as pltpu

F32 = jnp.float32
CDT = jnp.bfloat16

D = 2048
DEPTH = 2
H_MLA, NOPE, ROPE_MLA, V_MLA = 8, 128, 64, 128
Q_RANK, KV_RANK = 512, 256
H_DSA, G_DSA, HD_DSA = 8, 2, 128
ROT_DSA = HD_DSA // 4
H_IDX, D_IDX = 8, 64
ROT_IDX = D_IDX // 4
TOPK_MAX = 256
CONV_W, CONV_K = 1024, 3
THETA = 500000.0
N_EXPERTS, TOP_K = 8, 2
ALPHA = (2 * DEPTH) ** 0.25
LN_EPS = 1e-5
RMS_EPS = 1e-6
SPLITS = (Q_RANK, KV_RANK, ROPE_MLA, H_DSA * HD_DSA, G_DSA * HD_DSA, G_DSA * HD_DSA,
          H_IDX * D_IDX, D_IDX, H_IDX, CONV_W, CONV_W, CONV_W, 3 * D)

LANES = 128
NEG = -1e30
LOG2E = 1.4426950408889634
VMEM_LIMIT = 56 * 1024 * 1024

MOE_BLK = 512
MOE_GROUP = 2
MLA_BLK = 512


def _params(sem, vmem=VMEM_LIMIT):
    return pltpu.CompilerParams(dimension_semantics=sem, vmem_limit_bytes=vmem)


def _sigmoid(x):
    return 1.0 / (1.0 + jnp.exp(-x))


def _dot(a, b):
    return jnp.dot(a, b, preferred_element_type=F32)


def _dot_t(a, b):
    return lax.dot_general(a, b, (((1,), (1,)), ((), ())), preferred_element_type=F32)


def _layer_norm(y, g, b):
    mu = jnp.mean(y, -1, keepdims=True)
    d = y - mu
    var = jnp.mean(d * d, -1, keepdims=True)
    return d * lax.rsqrt(var + LN_EPS) * g + b


def _rope(y, c, s, half):
    lane = lax.broadcasted_iota(jnp.int32, y.shape, 1)
    swapped = jnp.where(lane < half, pltpu.roll(y, LANES - half, 1), pltpu.roll(y, half, 1))
    return y * c + swapped * s


def _ada_body(c_ref, w_ref, b_ref, o_ref):
    c = c_ref[...]
    ca = (c * _sigmoid(c)).astype(CDT)
    o_ref[...] = _dot(ca, w_ref[...].astype(CDT)) + b_ref[...]


def _ada(c, ada_w, ada_b):
    depth, d, n = ada_w.shape
    bsz = c.shape[0]
    rows = 8
    cp = jnp.zeros((rows, d), F32).at[:bsz].set(c)
    tn = 1024
    out = pl.pallas_call(
        _ada_body,
        grid=(depth, n // tn),
        in_specs=[pl.BlockSpec((rows, d), lambda l, j: (0, 0)),
                  pl.BlockSpec((None, d, tn), lambda l, j: (l, 0, j)),
                  pl.BlockSpec((None, 1, tn), lambda l, j: (l, 0, j))],
        out_specs=pl.BlockSpec((None, rows, tn), lambda l, j: (l, 0, j)),
        out_shape=jax.ShapeDtypeStruct((depth, rows, n), F32),
        compiler_params=_params(("arbitrary", "arbitrary")),
        name="ada_mod",
    )(cp, ada_w, ada_b.reshape(depth, 1, n))
    return out[:, :bsz]


def _mod_body(x_ref, sc_ref, sh_ref, o_ref):
    o_ref[...] = (x_ref[...] * (1.0 + sc_ref[...]) + sh_ref[...]).astype(o_ref.dtype)


def _modulate(x, sc, sh, out_dtype):
    bsz, s_len, d = x.shape
    tm = 512
    vec = pl.BlockSpec((None, 1, d), lambda b, i: (b, 0, 0))
    return pl.pallas_call(
        _mod_body,
        grid=(bsz, s_len // tm),
        in_specs=[pl.BlockSpec((None, tm, d), lambda b, i: (b, i, 0)), vec, vec],
        out_specs=pl.BlockSpec((None, tm, d), lambda b, i: (b, i, 0)),
        out_shape=jax.ShapeDtypeStruct(x.shape, out_dtype),
        compiler_params=_params(("arbitrary", "arbitrary")),
        name="modulate",
    )(x, sc, sh)


def _mla_prep_body(h_ref, w1_ref, qg_ref, kg_ref, wuq_ref, wukv_ref, c_ref, s_ref, q_ref, k_ref, v_ref):
    z = _dot(h_ref[...], w1_ref[...])
    cq = z[:, :Q_RANK]
    ckv = z[:, Q_RANK:Q_RANK + KV_RANK]
    kr = z[:, Q_RANK + KV_RANK:]
    nq = (cq * lax.rsqrt(jnp.mean(cq * cq, -1, keepdims=True) + RMS_EPS) * qg_ref[...]).astype(CDT)
    nkv = (ckv * lax.rsqrt(jnp.mean(ckv * ckv, -1, keepdims=True) + RMS_EPS) * kg_ref[...]).astype(CDT)
    qa = _dot(nq, wuq_ref[...])
    kva = _dot(nkv, wukv_ref[...])
    c = c_ref[...]
    s = s_ref[...]
    scale = (NOPE + ROPE_MLA) ** -0.5 * LOG2E
    krot_t = _rope(kr, c, s, ROPE_MLA // 2).T.astype(CDT)
    lane = lax.broadcasted_iota(jnp.int32, (h_ref.shape[0], LANES), 1)
    ones_col = jnp.where(lane == 0, 1.0, 0.0).astype(CDT)
    for h in range(H_MLA):
        lo = h * 2 * LANES
        q_ref[h, :, 0:LANES] = (qa[:, lo:lo + LANES] * scale).astype(CDT)
        q_ref[h, :, LANES:2 * LANES] = (_rope(qa[:, lo + LANES:lo + 2 * LANES], c, s, ROPE_MLA // 2)
                                        * scale).astype(CDT)
        k_ref[h, 0:LANES, :] = kva[:, lo:lo + LANES].T.astype(CDT)
        k_ref[h, LANES:2 * LANES, :] = krot_t
        v_ref[h, :, 0:LANES] = kva[:, lo + LANES:lo + 2 * LANES].astype(CDT)
        v_ref[h, :, LANES:2 * LANES] = ones_col


def _mla_prep(h, w1, qg, kg, wuq, wukv, ctab, stab):
    bsz, s_len, d = h.shape
    tm = min(MLA_BLK, s_len)
    full = lambda a: pl.BlockSpec(a.shape, lambda b, i: (0,) * a.ndim)
    row = lambda w: pl.BlockSpec((None, tm, w), lambda b, i: (b, i, 0))
    head = lambda w: pl.BlockSpec((None, H_MLA, tm, w), lambda b, i: (b, 0, i, 0))
    return pl.pallas_call(
        _mla_prep_body,
        grid=(bsz, s_len // tm),
        in_specs=[row(d), full(w1), full(qg), full(kg), full(wuq), full(wukv), row(LANES), row(LANES)],
        out_specs=[head(2 * LANES),
                   pl.BlockSpec((None, H_MLA, None, 2 * LANES, tm), lambda b, i: (b, 0, i, 0, 0)),
                   head(2 * LANES)],
        out_shape=[jax.ShapeDtypeStruct((bsz, H_MLA, s_len, 2 * LANES), CDT),
                   jax.ShapeDtypeStruct((bsz, H_MLA, s_len // tm, 2 * LANES, tm), CDT),
                   jax.ShapeDtypeStruct((bsz, H_MLA, s_len, 2 * LANES), CDT)],
        compiler_params=_params(("arbitrary", "arbitrary")),
        name="mla_prep",
    )(h, w1, qg, kg, wuq, wukv, ctab, stab)


def _flash_body(q_ref, k_ref, v_ref, o_ref, s_ref, mx_ref, m_ref, acc_ref, *, blk):
    qi = pl.program_id(2)
    dv = V_MLA

    def scores(hd, j, masked):
        s = _dot(q_ref[hd], k_ref[hd, j])
        if masked:
            row = lax.broadcasted_iota(jnp.int32, s.shape, 0)
            col = lax.broadcasted_iota(jnp.int32, s.shape, 1)
            s = jnp.where(col <= row, s, NEG)
        return s, jnp.max(s, -1, keepdims=True)

    def put(hd, sv):
        s_ref[hd], mx_ref[hd] = sv

    def accumulate(hd, j):
        off = pl.multiple_of(j * blk, blk)
        m_old = m_ref[hd]
        m_new = jnp.maximum(m_old, mx_ref[hd])
        alpha = jnp.exp2(m_old - m_new)
        p = jnp.exp2((s_ref[hd] - m_new).astype(CDT))
        acc_ref[hd] = alpha * acc_ref[hd] + _dot(p, v_ref[hd, pl.ds(off, blk), :])
        m_ref[hd] = m_new

    m_ref[...] = jnp.full(m_ref.shape, NEG, F32)
    acc_ref[...] = jnp.zeros(acc_ref.shape, F32)
    put(0, scores(0, qi, True))
    put(1, scores(1, qi, True))

    def body(n, _):
        j_prev = jnp.where(n == 1, qi, n - 2)
        accumulate(0, j_prev)
        accumulate(1, j_prev)
        put(0, scores(0, n - 1, False))
        put(1, scores(1, n - 1, False))
        return 0

    lax.fori_loop(1, qi + 1, body, 0)
    j_last = jnp.where(qi == 0, qi, qi - 1)
    accumulate(0, j_last)
    accumulate(1, j_last)
    for hd in range(2):
        acc = acc_ref[hd]
        o_ref[:, hd * dv:(hd + 1) * dv] = (acc[:, :dv] / acc[:, dv:dv + 1]).astype(o_ref.dtype)


def _mla_attention(q, k, v):
    bsz, nh, s_len, dq = q.shape
    dve = v.shape[-1]
    dv = V_MLA
    blk = min(MLA_BLK, s_len)
    return pl.pallas_call(
        functools.partial(_flash_body, blk=blk),
        grid=(bsz, nh // 2, s_len // blk),
        in_specs=[pl.BlockSpec((None, 2, blk, dq), lambda b, h, i: (b, h, i, 0)),
                  pl.BlockSpec((None, 2, s_len // blk, dq, blk), lambda b, h, i: (b, h, 0, 0, 0)),
                  pl.BlockSpec((None, 2, s_len, dve), lambda b, h, i: (b, h, 0, 0))],
        out_specs=pl.BlockSpec((None, blk, 2 * dv), lambda b, h, i: (b, i, h)),
        out_shape=jax.ShapeDtypeStruct((bsz, s_len, nh * dv), CDT),
        scratch_shapes=[pltpu.VMEM((2, blk, blk), F32), pltpu.VMEM((2, blk, 1), F32),
                        pltpu.VMEM((2, blk, 1), F32), pltpu.VMEM((2, blk, dve), F32)],
        compiler_params=_params(("arbitrary", "arbitrary", "arbitrary")),
        name="mla_attention",
    )(q, k, v)


DSA_CHUNK = 512
DSA_TQ = 128
VT_PAD = 16
VT_ROWS = HD_DSA + VT_PAD


def _dsa_prep_body(h_ref, w2_ref, cd_ref, sd_ref, ci_ref, si_ref, q_ref, k_ref, vt_ref, qi_ref, ki_ref, wit_ref):
    z = _dot(h_ref[...], w2_ref[...])
    cd, sd, ci, si = cd_ref[...], sd_ref[...], ci_ref[...], si_ref[...]
    col = lambda j: z[:, j * LANES:(j + 1) * LANES]
    for h in range(H_DSA):
        q_ref[h] = (_rope(col(h), cd, sd, ROT_DSA // 2) * (HD_DSA ** -0.5 * LOG2E)).T.astype(CDT)
    for g in range(G_DSA):
        k_ref[:, g * LANES:(g + 1) * LANES] = _rope(col(H_DSA + g), cd, sd, ROT_DSA // 2).astype(CDT)
    base = H_DSA + G_DSA
    sub = lax.broadcasted_iota(jnp.int32, (VT_PAD, z.shape[0]), 0)
    ones_row = jnp.where(sub == 0, 1.0, 0.0).astype(CDT)
    for g in range(G_DSA):
        vt_ref[g * VT_ROWS:g * VT_ROWS + LANES, :] = col(base + g).T.astype(CDT)
        vt_ref[g * VT_ROWS + LANES:(g + 1) * VT_ROWS, :] = ones_row
    base += G_DSA
    for h in range(H_IDX):
        qi_ref[h] = (_rope(col(base + h), ci, si, ROT_IDX // 2) * D_IDX ** -0.5).T.astype(CDT)
    base += H_IDX
    ki_ref[...] = _rope(col(base), ci, si, ROT_IDX // 2).astype(CDT)
    wit_ref[...] = col(base + 1).T[0:H_IDX, :] * H_IDX ** -0.5


def _dsa_prep(h, w2, cd, sd, ci, si):
    bsz, s_len, d = h.shape
    tm = min(DSA_CHUNK, s_len)
    row = lambda w: pl.BlockSpec((None, tm, w), lambda b, i: (b, i, 0))
    return pl.pallas_call(
        _dsa_prep_body,
        grid=(bsz, s_len // tm),
        in_specs=[row(d), pl.BlockSpec(w2.shape, lambda b, i: (0, 0)), row(LANES), row(LANES), row(LANES),
                  row(LANES)],
        out_specs=[pl.BlockSpec((None, H_DSA, LANES, tm), lambda b, i: (b, 0, 0, i)), row(G_DSA * LANES),
                   pl.BlockSpec((None, None, G_DSA * VT_ROWS, tm), lambda b, i: (b, i, 0, 0)),
                   pl.BlockSpec((None, H_IDX, LANES, tm), lambda b, i: (b, 0, 0, i)), row(LANES),
                   pl.BlockSpec((None, H_IDX, tm), lambda b, i: (b, 0, i))],
        out_shape=[jax.ShapeDtypeStruct((bsz, H_DSA, LANES, s_len), CDT),
                   jax.ShapeDtypeStruct((bsz, s_len, G_DSA * LANES), CDT),
                   jax.ShapeDtypeStruct((bsz, s_len // tm, G_DSA * VT_ROWS, tm), CDT),
                   jax.ShapeDtypeStruct((bsz, H_IDX, LANES, s_len), CDT),
                   jax.ShapeDtypeStruct((bsz, s_len, LANES), CDT),
                   jax.ShapeDtypeStruct((bsz, H_IDX, s_len), F32)],
        compiler_params=_params(("arbitrary", "arbitrary")),
        name="dsa_prep",
    )(h, w2, cd, sd, ci, si)


FOLD_ROWS = 64


def _fold(x):
    return jnp.sum(x.reshape(x.shape[0] // FOLD_ROWS, FOLD_ROWS, x.shape[1]), axis=0)


def _dsa_body(qi_ref, ki_ref, wit_ref, q_ref, k_ref, vt_ref, tri_ref, o_ref,
              key_ref, sel_ref, qia_ref, qg_ref, s_ref, mx_ref, m_ref, acc_ref, *, tq, tkc, n_sel):
    i = pl.program_id(1)
    q0 = i * tq
    nkc = (q0 + tq + tkc - 1) // tkc
    n_rep = H_DSA // G_DSA
    key_neg_inf = jnp.int32(-8388608) ^ jnp.int32(0x7FFFFFFF)
    qpos = q0 + lax.broadcasted_iota(jnp.int32, (tkc, tq), 1)

    for h in range(H_IDX):
        qia_ref[:, h * tq:(h + 1) * tq] = qi_ref[h]
    for g in range(G_DSA):
        for r in range(n_rep):
            qg_ref[g, :, r * tq:(r + 1) * tq] = q_ref[g * n_rep + r]

    def score_chunk(c, _):
        off = pl.multiple_of(c * tkc, tkc)
        kc = ki_ref[pl.ds(off, tkc), :]
        w = wit_ref[...]
        lg = _dot(kc, qia_ref[...])
        sc = jnp.zeros((tkc, tq), F32)
        for h in range(H_IDX):
            sc = sc + w[h:h + 1, :] * jnp.maximum(lg[:, h * tq:(h + 1) * tq], 0.0)
        kpos = off + lax.broadcasted_iota(jnp.int32, (tkc, tq), 0)
        sc = jnp.where(kpos <= qpos, sc, -jnp.inf)
        bits = pltpu.bitcast(sc, jnp.int32)
        key_ref[c] = jnp.where(bits >= 0, bits, bits ^ jnp.int32(0x7FFFFFFF))
        return 0

    lax.fori_loop(0, nkc, score_chunk, 0)

    def count(pred):
        def body(c, cnt):
            return cnt + _fold(jnp.where(pred(key_ref[c]), 1.0, 0.0))
        part = lax.fori_loop(0, nkc, body, jnp.zeros((FOLD_ROWS, tq), F32))
        return jnp.sum(part, axis=0, keepdims=True)

    def bit_step(bi, thr):
        cand = thr + lax.shift_left(jnp.int32(1), 31 - bi)
        cnt = count(lambda kk: kk >= cand)
        return jnp.where(cnt >= n_sel, cand, thr)

    thr = lax.fori_loop(0, 32, bit_step, jnp.full((1, tq), jnp.iinfo(jnp.int32).min, jnp.int32))

    cnt_gt = count(lambda kk: kk > thr)
    cnt_eq = count(lambda kk: kk == thr)
    need = n_sel - cnt_gt
    excess = jnp.where(thr > key_neg_inf, cnt_eq - need, 0.0)

    def mask_plain():
        def body(c, _):
            kk = key_ref[c]
            sel_ref[c] = jnp.where(kk >= thr, jnp.where(kk > key_neg_inf, 0.0, NEG), NEG)
            return 0
        lax.fori_loop(0, nkc, body, 0)

    def mask_ranked():
        def body(c, seen):
            kk = key_ref[c]
            tie = kk == thr
            pre = _dot(tri_ref[...], jnp.where(tie, 1.0, 0.0).astype(CDT))
            rank = seen + pre
            keep_tie = jnp.where(tie, jnp.where(rank <= need, 0.0, NEG), NEG)
            keep = jnp.where(kk > thr, 0.0, keep_tie)
            sel_ref[c] = jnp.where(kk > key_neg_inf, keep, NEG)
            return seen + pre[tkc - 1:tkc, :]
        lax.fori_loop(0, nkc, body, jnp.zeros((1, tq), F32))

    lax.cond(jnp.max(excess) > 0.0, mask_ranked, mask_plain)

    m_ref[...] = jnp.full(m_ref.shape, NEG, F32)
    acc_ref[...] = jnp.zeros(acc_ref.shape, F32)

    def scores(g, c):
        off = pl.multiple_of(c * tkc, tkc)
        msk = sel_ref[c]
        msk = jnp.concatenate([msk] * n_rep, axis=1)
        kg = k_ref[pl.ds(off, tkc), g * LANES:(g + 1) * LANES]
        s = _dot(kg, qg_ref[g]) + msk
        return s, jnp.max(s, axis=0, keepdims=True)

    def put(g, sv):
        s_ref[g], mx_ref[g] = sv

    def accumulate(g, c):
        m_old = m_ref[g]
        m_new = jnp.maximum(m_old, mx_ref[g])
        alpha = jnp.exp2(m_old - m_new)
        p = jnp.exp2((s_ref[g] - m_new).astype(CDT))
        acc_ref[g] = alpha * acc_ref[g] + _dot(vt_ref[c, g * VT_ROWS:(g + 1) * VT_ROWS, :], p)
        m_ref[g] = m_new

    put(0, scores(0, 0))
    put(1, scores(1, 0))

    def att_chunk(c, _):
        sv0 = scores(0, c)
        sv1 = scores(1, c)
        accumulate(0, c - 1)
        accumulate(1, c - 1)
        put(0, sv0)
        put(1, sv1)
        return 0

    lax.fori_loop(1, nkc, att_chunk, 0)
    accumulate(0, nkc - 1)
    accumulate(1, nkc - 1)
    for g in range(G_DSA):
        acc = acc_ref[g]
        out = acc[0:HD_DSA, :] / acc[HD_DSA:HD_DSA + 1, :]
        for r in range(n_rep):
            hh = g * n_rep + r
            o_ref[:, hh * LANES:(hh + 1) * LANES] = out[:, r * tq:(r + 1) * tq].T.astype(o_ref.dtype)


def _dsa_attention(qi, ki, wit, q, k, vt):
    bsz, s_len, _ = k.shape
    tq = min(DSA_TQ, s_len)
    tkc = min(DSA_CHUNK, s_len)
    n_chunks = s_len // tkc
    n_sel = min(TOPK_MAX, s_len // 4)
    n_rep = H_DSA // G_DSA
    tri =(lax.broadcasted_iota(jnp.int32, (tkc, tkc), 1)
           <= lax.broadcasted_iota(jnp.int32, (tkc, tkc), 0)).astype(CDT)
    row = lambda w: pl.BlockSpec((None, tq, w), lambda b, i: (b, i, 0))
    seq = lambda w: pl.BlockSpec((None, s_len, w), lambda b, i: (b, 0, 0))
    return pl.pallas_call(
        functools.partial(_dsa_body, tq=tq, tkc=tkc, n_sel=n_sel),
        grid=(bsz, s_len // tq),
        in_specs=[pl.BlockSpec((None, H_IDX, LANES, tq), lambda b, i: (b, 0, 0, i)), seq(LANES),
                  pl.BlockSpec((None, H_IDX, tq), lambda b, i: (b, 0, i)),
                  pl.BlockSpec((None, H_DSA, LANES, tq), lambda b, i: (b, 0, 0, i)), seq(G_DSA * LANES),
                  pl.BlockSpec((None, n_chunks, G_DSA * VT_ROWS, tkc), lambda b, i: (b, 0, 0, 0)),
                  pl.BlockSpec((tkc, tkc), lambda b, i: (0, 0))],
        out_specs=row(H_DSA * LANES),
        out_shape=jax.ShapeDtypeStruct((bsz, s_len, H_DSA * LANES), CDT),
        scratch_shapes=[pltpu.VMEM((n_chunks, tkc, tq), jnp.int32),
                        pltpu.VMEM((n_chunks, tkc, tq), F32),
                        pltpu.VMEM((LANES, H_IDX * tq), CDT),
                        pltpu.VMEM((G_DSA, LANES, n_rep * tq), CDT),
                        pltpu.VMEM((G_DSA, tkc, n_rep * tq), F32),
                        pltpu.VMEM((G_DSA, 1, n_rep * tq), F32),
                        pltpu.VMEM((G_DSA, 1, n_rep * tq), F32),
                        pltpu.VMEM((G_DSA, VT_ROWS, n_rep * tq), F32)],
        compiler_params=_params(("arbitrary", "arbitrary")),
        name="dsa_attention",
    )(qi, ki, wit, q, k, vt, tri)


def _conv_body(h_ref, w_ref, cw_ref, cb_ref, o_ref, tail_ref, *, tm, tc):
    @pl.when(pl.program_id(2) == 0)
    def _():
        tail_ref[...] = jnp.zeros(tail_ref.shape, F32)

    z = _dot(h_ref[...], w_ref[...])
    gb = z[:, :tc]
    u = z[:, tc:2 * tc] * z[:, 2 * tc:]
    prev = tail_ref[...]
    row = lax.broadcasted_iota(jnp.int32, (tm, tc), 0)
    u1 = jnp.where(row == 0, prev[7:8, :], pltpu.roll(u, 1, 0))
    u2 = jnp.where(row == 0, prev[6:7, :], jnp.where(row == 1, prev[7:8, :], pltpu.roll(u, 2, 0)))
    cw = cw_ref[...]
    y = cw[0:1, :] * u2 + cw[1:2, :] * u1 + cw[2:3, :] * u + cb_ref[...]
    o_ref[...] = (gb * y).astype(o_ref.dtype)
    tail_ref[...] = u[tm - 8:tm, :]


def _conv_branch(h, w3, conv_w, conv_b, tc):
    bsz, s_len, d = h.shape
    tm = 512
    n_cb = CONV_W // tc
    return pl.pallas_call(
        functools.partial(_conv_body, tm=tm, tc=tc),
        grid=(n_cb, bsz, s_len // tm),
        in_specs=[pl.BlockSpec((None, tm, d), lambda cb, b, i: (b, i, 0)),
                  pl.BlockSpec((d, 3 * tc), lambda cb, b, i: (0, cb)),
                  pl.BlockSpec((CONV_K, tc), lambda cb, b, i: (0, cb)),
                  pl.BlockSpec((1, tc), lambda cb, b, i: (0, cb))],
        out_specs=pl.BlockSpec((None, tm, tc), lambda cb, b, i: (b, i, cb)),
        out_shape=jax.ShapeDtypeStruct((bsz, s_len, CONV_W), CDT),
        scratch_shapes=[pltpu.VMEM((8, tc), F32)],
        compiler_params=_params(("arbitrary", "arbitrary", "arbitrary")),
        name="conv_branch",
    )(h, w3, conv_w, conv_b.reshape(1, CONV_W))


def _merge_body(h_ref, ya_ref, yb_ref, yc_ref, wg_ref, wbr_ref, o_ref):
    h = h_ref[...]
    acc = None
    for j, y_ref in enumerate((ya_ref, yb_ref, yc_ref)):
        term = _sigmoid(_dot(h, wg_ref[j])) * _dot(y_ref[...], wbr_ref[j])
        acc = term if acc is None else acc + term
    o_ref[...] = acc.astype(o_ref.dtype)


def _merge(h, ya, yb, yc, wg, wbr):
    t, d = h.shape
    tm, tn = 512, 512
    yw = ya.shape[1]
    row = lambda w: pl.BlockSpec((tm, w), lambda j, i: (i, 0))
    return pl.pallas_call(
        _merge_body,
        grid=(d // tn, t // tm),
        in_specs=[row(d), row(yw), row(yw), row(yw),
                  pl.BlockSpec((3, d, tn), lambda j, i: (0, 0, j)),
                  pl.BlockSpec((3, yw, tn), lambda j, i: (0, 0, j))],
        out_specs=pl.BlockSpec((tm, tn), lambda j, i: (i, j)),
        out_shape=jax.ShapeDtypeStruct((t, d), CDT),
        compiler_params=_params(("arbitrary", "arbitrary")),
        name="gated_merge",
    )(h, ya, yb, yc, wg, wbr)


def _out_ln_body(m_ref, wo_ref, x_ref, gm_ref, lg_ref, lb_ref, sc_ref, sh_ref, x1_ref, h_ref):
    y = ALPHA * x_ref[...] + gm_ref[...] * _dot(m_ref[...], wo_ref[...])
    x1 = _layer_norm(y, lg_ref[...], lb_ref[...])
    x1_ref[...] = x1
    h_ref[...] = (x1 * (1.0 + sc_ref[...]) + sh_ref[...]).astype(h_ref.dtype)


def _out_ln(merged, wo, x, gm, lg, lb, sc, sh, h_dtype):
    bsz, s_len, d = x.shape
    tm = 256
    row = pl.BlockSpec((None, tm, d), lambda b, i: (b, i, 0))
    per_b = pl.BlockSpec((None, 1, d), lambda b, i: (b, 0, 0))
    vec = pl.BlockSpec((1, d), lambda b, i: (0, 0))
    return pl.pallas_call(
        _out_ln_body,
        grid=(bsz, s_len // tm),
        in_specs=[row, pl.BlockSpec((d, d), lambda b, i: (0, 0)), row, per_b, vec, vec, per_b, per_b],
        out_specs=[row, row],
        out_shape=[jax.ShapeDtypeStruct(x.shape, F32), jax.ShapeDtypeStruct(x.shape, h_dtype)],
        compiler_params=_params(("arbitrary", "arbitrary")),
        name="out_proj_ln",
    )(merged, wo, x, gm, lg, lb, sc, sh)


def _ffn_body(h_ref, wg_ref, wu_ref, wd_ref, x_ref, gf_ref, lg_ref, lb_ref, sc_ref, sh_ref, x2_ref, h2_ref,
              acc_ref):
    k = pl.program_id(2)

    @pl.when(k == 0)
    def _():
        acc_ref[...] = jnp.zeros(acc_ref.shape, F32)

    h = h_ref[...]
    a = _dot(h, wg_ref[...])
    act = (a * _sigmoid(a) * _dot(h, wu_ref[...])).astype(CDT)
    acc_ref[...] += _dot(act, wd_ref[...])

    @pl.when(k == pl.num_programs(2) - 1)
    def _():
        y = ALPHA * x_ref[...] + gf_ref[...] * acc_ref[...]
        x2 = _layer_norm(y, lg_ref[...], lb_ref[...])
        x2_ref[...] = x2
        h2_ref[...] = (x2 * (1.0 + sc_ref[...]) + sh_ref[...]).astype(h2_ref.dtype)


def _ffn_ln(h, wg, wu, wd, x, gf, lg, lb, sc, sh):
    bsz, s_len, d = x.shape
    f = wg.shape[1]
    tm, tf = 512, 512
    row = pl.BlockSpec((None, tm, d), lambda b, i, k: (b, i, 0))
    per_b = pl.BlockSpec((None, 1, d), lambda b, i, k: (b, 0, 0))
    vec = pl.BlockSpec((1, d), lambda b, i, k: (0, 0))
    return pl.pallas_call(
        _ffn_body,
        grid=(bsz, s_len // tm, f // tf),
        in_specs=[row, pl.BlockSpec((d, tf), lambda b, i, k: (0, k)), pl.BlockSpec((d, tf), lambda b, i, k: (0, k)),
                  pl.BlockSpec((tf, d), lambda b, i, k: (k, 0)), row, per_b, vec, vec, per_b, per_b],
        out_specs=[row, row],
        out_shape=[jax.ShapeDtypeStruct(x.shape, F32), jax.ShapeDtypeStruct(x.shape, CDT)],
        scratch_shapes=[pltpu.VMEM((tm, d), F32)],
        compiler_params=_params(("arbitrary", "arbitrary", "arbitrary")),
        name="ffn_ln",
    )(h, wg, wu, wd, x, gf, lg, lb, sc, sh)


def _router_body(h_ref, wr_ref, e_ref, g_ref):
    lg = jnp.dot(h_ref[...], wr_ref[...], preferred_element_type=F32, precision=lax.Precision.HIGHEST)
    lane = lax.broadcasted_iota(jnp.int32, lg.shape, 1)
    lanef = lane.astype(F32)
    lg = jnp.where(lane < N_EXPERTS, lg, -jnp.inf)
    m1 = jnp.max(lg, -1, keepdims=True)
    i1 = jnp.min(jnp.where(lg == m1, lanef, float(LANES)), -1, keepdims=True)
    lg2 = jnp.where(lanef == i1, -jnp.inf, lg)
    m2 = jnp.max(lg2, -1, keepdims=True)
    i2 = jnp.min(jnp.where(lg2 == m2, lanef, float(LANES)), -1, keepdims=True)
    e = jnp.exp(m2 - m1)
    g1 = 1.0 / (1.0 + e)
    g2 = e / (1.0 + e)
    e_ref[...] = jnp.where(lane == 0, i1, jnp.where(lane == 1, i2, 0.0)).astype(jnp.int32)
    g_ref[...] = jnp.where(lane == 0, g1, jnp.where(lane == 1, g2, 0.0))


def _router(h, wr):
    t, d = h.shape
    tm = 512
    return pl.pallas_call(
        _router_body,
        grid=(t // tm,),
        in_specs=[pl.BlockSpec((tm, d), lambda i: (i, 0)), pl.BlockSpec((d, LANES), lambda i: (0, 0))],
        out_specs=[pl.BlockSpec((tm, LANES), lambda i: (i, 0)), pl.BlockSpec((tm, LANES), lambda i: (i, 0))],
        out_shape=[jax.ShapeDtypeStruct((t, LANES), jnp.int32), jax.ShapeDtypeStruct((t, LANES), F32)],
        compiler_params=_params(("arbitrary",)),
        name="moe_router",
    )(h, wr)


def _moe_body(tok_ref, ge_ref, bv_ref, h_hbm, wg_ref, wu_ref, wd_ref, y_hbm, xbuf, xs, y_ref, sem, out_sem, *,
              blk, per, n_steps):
    i = pl.program_id(0)
    k = pl.program_id(1)
    ng = pl.num_programs(0)
    nk = pl.num_programs(1)
    rows = MOE_GROUP * blk

    def out_copy(gi):
        return pltpu.make_async_copy(y_ref, y_hbm.at[pl.ds(pl.multiple_of(gi * rows, rows), rows), :], out_sem)

    def row_copy(gi, r):
        return pltpu.make_async_copy(h_hbm.at[pl.ds(tok_ref[gi * rows + r], 1), :], xbuf.at[pl.ds(r, 1), :], sem)

    def wait_rows():
        pltpu.make_async_copy(h_hbm.at[pl.ds(0, per * n_steps), :], xbuf, sem).wait()

    @pl.when(k == 0)
    def _():
        @pl.when(i == 0)
        def _():
            def body(r, _):
                row_copy(0, r).start()
                return 0
            lax.fori_loop(0, per * n_steps, body, 0)
            wait_rows()

        @pl.when((i > 0) & (bv_ref[jnp.maximum(i - 1, 0) * MOE_GROUP] == 1))
        def _():
            wait_rows()

        xs[...] = xbuf[0:rows, :].astype(CDT)

        @pl.when(i > 0)
        def _():
            out_copy(i - 1).wait()

        y_ref[...] = jnp.zeros(y_ref.shape, F32)

    def ffn(n_blocks):
        for j in range(per):
            row_copy(i + 1, k * per + j).start()
        wg = wg_ref[...].astype(CDT)
        wu = wu_ref[...].astype(CDT)
        wd = wd_ref[...].astype(CDT)
        for b in range(n_blocks):
            x = xs[b * blk:(b + 1) * blk, :]
            a = _dot(x, wg)
            act = (a * _sigmoid(a) * _dot(x, wu)).astype(CDT)
            y_ref[b * blk:(b + 1) * blk, :] += _dot(act, wd)

    live = bv_ref[i * MOE_GROUP]
    for b in range(1, MOE_GROUP):
        live = live + bv_ref[i * MOE_GROUP + b]
    for n_blocks in range(1, MOE_GROUP + 1):
        pl.when(live == n_blocks)(functools.partial(ffn, n_blocks))

    @pl.when(k == nk - 1)
    def _():
        out_copy(i).start()

        @pl.when(i == ng - 1)
        def _():
            out_copy(i).wait()


def _moe_ffn(h, row_tok, grp_e, blk_valid, wg, wu, wd):
    t, d = h.shape
    n_rows = row_tok.shape[0]
    blk = MOE_BLK
    rows = MOE_GROUP * blk
    ng = n_rows // rows
    f = wg.shape[2]
    tf = 512
    nk = f // tf
    sublanes = 8
    per = -(-(-(-rows // nk)) // sublanes) * sublanes
    row_tok = jnp.pad(row_tok, (0, per * nk - rows))
    kk = lambda k, i, bv: jnp.where(bv[i * MOE_GROUP] == 1, k, nk - 1)
    grid_spec = pltpu.PrefetchScalarGridSpec(
        num_scalar_prefetch=3,
        grid=(ng, nk),
        in_specs=[pl.BlockSpec(memory_space=pl.ANY),
                  pl.BlockSpec((None, d, tf), lambda i, k, tok, ge, bv: (ge[i], 0, kk(k, i, bv))),
                  pl.BlockSpec((None, d, tf), lambda i, k, tok, ge, bv: (ge[i], 0, kk(k, i, bv))),
                  pl.BlockSpec((None, tf, d), lambda i, k, tok, ge, bv: (ge[i], kk(k, i, bv), 0))],
        out_specs=pl.BlockSpec(memory_space=pl.ANY),
        scratch_shapes=[pltpu.VMEM((per * nk, d), F32), pltpu.VMEM((rows, d), CDT), pltpu.VMEM((rows, d), F32),
                        pltpu.SemaphoreType.DMA(()), pltpu.SemaphoreType.DMA(())],
    )
    return pl.pallas_call(
        functools.partial(_moe_body, blk=blk, per=per, n_steps=nk),
        grid_spec=grid_spec,
        out_shape=jax.ShapeDtypeStruct((n_rows, d), F32),
        compiler_params=_params(("arbitrary", "arbitrary")),
        name="moe_grouped_ffn",
    )(row_tok, grp_e, blk_valid, h, wg, wu, wd)


def _combine_body(pos_ref, y_hbm, gate_ref, x_ref, gf_ref, lg_ref, lb_ref, o_ref, ybuf, sem, *, tm):
    i = pl.program_id(0)
    nb = pl.num_programs(0)

    def issue(bi, slot):
        def body(r, _):
            for s in range(TOP_K):
                p = pos_ref[(bi * tm + r) * TOP_K + s]
                pltpu.make_async_copy(y_hbm.at[pl.ds(p, 1), :], ybuf.at[slot, s, pl.ds(r, 1), :],
                                      sem.at[slot]).start()
            return 0
        lax.fori_loop(0, tm, body, 0)

    @pl.when(i == 0)
    def _():
        issue(0, 0)

    slot = i % 2
    for s in range(TOP_K):
        pltpu.make_async_copy(y_hbm.at[pl.ds(0, tm), :], ybuf.at[slot, s], sem.at[slot]).wait()

    @pl.when(i + 1 < nb)
    def _():
        issue(i + 1, (i + 1) % 2)

    gate = gate_ref[...]
    f = ybuf[slot, 0] * gate[:, 0:1] + ybuf[slot, 1] * gate[:, 1:2]
    y = ALPHA * x_ref[...] + gf_ref[...] * f
    o_ref[...] = _layer_norm(y, lg_ref[...], lb_ref[...])


def _combine_ln(y_rows, pos, gate, x, gf, lg, lb):
    bsz, s_len, d = x.shape
    tm = 256
    nsb = s_len // tm
    grid_spec = pltpu.PrefetchScalarGridSpec(
        num_scalar_prefetch=1,
        grid=(bsz * nsb,),
        in_specs=[pl.BlockSpec(memory_space=pl.ANY),
                  pl.BlockSpec((tm, LANES), lambda i, pos: (i, 0)),
                  pl.BlockSpec((None, tm, d), lambda i, pos: (i // nsb, i % nsb, 0)),
                  pl.BlockSpec((None, 1, d), lambda i, pos: (i // nsb, 0, 0)),
                  pl.BlockSpec((1, d), lambda i, pos: (0, 0)),
                  pl.BlockSpec((1, d), lambda i, pos: (0, 0))],
        out_specs=pl.BlockSpec((None, tm, d), lambda i, pos: (i // nsb, i % nsb, 0)),
        scratch_shapes=[pltpu.VMEM((2, TOP_K, tm, d), F32), pltpu.SemaphoreType.DMA((2,))],
    )
    return pl.pallas_call(
        functools.partial(_combine_body, tm=tm),
        grid_spec=grid_spec,
        out_shape=jax.ShapeDtypeStruct(x.shape, F32),
        compiler_params=_params(("arbitrary",)),
        name="moe_combine_ln",
    )(pos, y_rows, gate, x, gf, lg, lb)


def _routing_tables(top_e, n_tok):
    blk = MOE_BLK
    rows = MOE_GROUP * blk
    flat_e = top_e.reshape(-1)
    onehot = (flat_e[:, None] == jnp.arange(N_EXPERTS, dtype=jnp.int32)[None, :]).astype(jnp.int32)
    csum = jnp.cumsum(onehot, axis=0)
    counts = csum[-1]
    rank = jnp.sum(csum * onehot, axis=1) - 1
    padded = (counts + rows - 1) // rows * rows
    pend = jnp.cumsum(padded)
    pstart = pend - padded
    pos = (jnp.sum(pstart[None, :] * onehot, axis=1) + rank).astype(jnp.int32)
    n_rows = ((n_tok * TOP_K + N_EXPERTS * (rows - 1)) // rows + 1) * rows
    flat_tok = jnp.repeat(jnp.arange(n_tok, dtype=jnp.int32), TOP_K)
    row_tok = jnp.zeros((n_rows,), jnp.int32).at[pos].set(flat_tok)
    grp_start = jnp.arange(n_rows // rows, dtype=jnp.int32) * rows
    last_start = jnp.maximum(pend[-1] - rows, 0)
    grp_e = jnp.sum((pend[None, :] <= jnp.minimum(grp_start, last_start)[:, None]).astype(jnp.int32), axis=1)
    grp_e = jnp.minimum(grp_e, N_EXPERTS - 1)
    blk_start = jnp.arange(n_rows // blk, dtype=jnp.int32) * blk
    blk_e = jnp.repeat(grp_e, MOE_GROUP)
    blk_valid = ((blk_start < pend[-1]) & (blk_start - pstart[blk_e] < counts[blk_e])).astype(jnp.int32)
    return pos, row_tok, grp_e, blk_valid


def _rope_tables(positions, rot_dim, pad_value):
    half = rot_dim // 2
    inv = jnp.asarray(THETA ** (-np.arange(0, rot_dim, 2, dtype=np.float32) / np.float32(rot_dim)), F32)
    ang = positions.astype(F32)[..., None] * inv
    cos, sin = jnp.cos(ang), jnp.sin(ang)
    rest = positions.shape + (LANES - 2 * half,)
    c = jnp.concatenate([cos, cos, jnp.full(rest, pad_value, F32)], -1)
    s = jnp.concatenate([-sin, sin, jnp.zeros(rest, F32)], -1)
    return c, s


def _pad_cols(w, width):
    return jnp.pad(w, ((0, 0), (0, width - w.shape[1])))


def _mixer_weights(w_in, w_uq, w_ukv, w_a, w_b, w_c, tc):
    offs = [0]
    for n in SPLITS:
        offs.append(offs[-1] + n)
    part = lambda j: w_in[:, offs[j]:offs[j + 1]]
    d = w_in.shape[0]
    w1 = _pad_cols(jnp.concatenate([part(0), part(1), part(2)], 1), Q_RANK + KV_RANK + LANES)
    qi = jnp.pad(part(6).reshape(d, H_IDX, D_IDX), ((0, 0), (0, 0), (0, LANES - D_IDX))).reshape(d, H_IDX * LANES)
    w2 = jnp.concatenate([part(3), part(4), part(5), qi, _pad_cols(part(7), LANES), _pad_cols(part(8), LANES)], 1)
    n_cb = CONV_W // tc
    w3 = jnp.stack([part(9).reshape(d, n_cb, tc), part(10).reshape(d, n_cb, tc), part(11).reshape(d, n_cb, tc)],
                   axis=2).reshape(d, 3 * CONV_W)
    wg = part(12).reshape(d, 3, d).transpose(1, 0, 2)
    wuq = jnp.pad(w_uq.reshape(Q_RANK, H_MLA, NOPE + ROPE_MLA),
                  ((0, 0), (0, 0), (0, 2 * LANES - NOPE - ROPE_MLA))).reshape(Q_RANK, H_MLA * 2 * LANES)
    wbr = jnp.stack([w_a, w_b, w_c], 0)
    cast = lambda a: a.astype(CDT)
    return cast(w1), cast(w2), cast(w3), cast(wg), cast(wuq), cast(w_ukv), cast(wbr)


def kernel(x, c, positions, ada_w, ada_b, ln1_g, ln1_b, ln2_g, ln2_b, w_in, mla_q_norm, mla_kv_norm, w_uq, w_ukv,
           conv_w, conv_b, w_branch_a, w_branch_b, w_branch_c, w_o, ffn_w_gate, ffn_w_up, ffn_w_down, router_w,
           moe_w_gate, moe_w_up, moe_w_down):
    bsz, s_len, d = x.shape
    n_tok = bsz * s_len
    depth = ada_w.shape[0]
    conv_tc = 512

    cm, sm = _rope_tables(positions, ROPE_MLA, 0.0)
    cd, sd = _rope_tables(positions, ROT_DSA, 1.0)
    ci, si = _rope_tables(positions, ROT_IDX, 1.0)

    mod = _ada(c, ada_w, ada_b)
    vec = lambda a: a.reshape(1, d)

    h = None
    for i in range(depth):
        sh_m, sc_m, g_m, sh_f, sc_f, g_f = [mod[i, :, j * d:(j + 1) * d].reshape(bsz, 1, d) for j in range(6)]
        if h is None:
            h = _modulate(x, sc_m, sh_m, CDT)
        w1, w2, w3, wg, wuq, wukv, wbr = _mixer_weights(w_in[i], w_uq[i], w_ukv[i], w_branch_a[i], w_branch_b[i],
                                                       w_branch_c[i], conv_tc)

        qa, ka, va = _mla_prep(h, w1, mla_q_norm[i].reshape(1, -1), mla_kv_norm[i].reshape(1, -1), wuq, wukv, cm, sm)
        ya = _mla_attention(qa, ka, va)

        qd, kd, vd, qi, ki, wi = _dsa_prep(h, w2, cd, sd, ci, si)
        yb = _dsa_attention(qi, ki, wi, qd, kd, vd)

        yc = _conv_branch(h, w3, conv_w[i], conv_b[i], conv_tc)

        merged = _merge(h.reshape(n_tok, d), ya.reshape(n_tok, -1), yb.reshape(n_tok, -1), yc.reshape(n_tok, -1),
                        wg, wbr)
        moe_layer = i % 2 == 1
        x, h = _out_ln(merged.reshape(bsz, s_len, d), w_o[i].astype(CDT), x, g_m, vec(ln1_g[i]), vec(ln1_b[i]),
                       sc_f, sh_f, F32 if moe_layer else CDT)

        j = i // 2
        if i + 1 < depth:
            sh_n, sc_n = [mod[i + 1, :, q * d:(q + 1) * d].reshape(bsz, 1, d) for q in range(2)]
        else:
            sh_n, sc_n = jnp.zeros((bsz, 1, d), F32), jnp.zeros((bsz, 1, d), F32)
        if not moe_layer:
            x, h = _ffn_ln(h, ffn_w_gate[j].astype(CDT), ffn_w_up[j].astype(CDT), ffn_w_down[j].astype(CDT), x, g_f,
                           vec(ln2_g[i]), vec(ln2_b[i]), sc_n, sh_n)
        else:
            ht = h.reshape(n_tok, d)
            top_e, top_g = _router(ht, _pad_cols(router_w[j], LANES))
            pos, row_tok, grp_e, blk_valid = _routing_tables(top_e[:, :TOP_K], n_tok)
            y_rows = _moe_ffn(ht, row_tok, grp_e, blk_valid, moe_w_gate[j], moe_w_up[j], moe_w_down[j])
            x = _combine_ln(y_rows, pos, top_g, x, g_f, vec(ln2_g[i]), vec(ln2_b[i]))
            h = None if i + 1 >= depth else _modulate(x, sc_n, sh_n, CDT)
    return x
```

```python
import functools

import jax
import jax.numpy as jnp
import numpy as np
from jax import lax
from jax.experimental import pallas as pl
from jax.experimental.pallas import tpu as pltpu

F32 = jnp.float32
CDT = jnp.bfloat16

D = 2048
DEPTH = 2
H_MLA, NOPE, ROPE_MLA, V_MLA = 8, 128, 64, 128
Q_RANK, KV_RANK = 512, 256
H_DSA, G_DSA, HD_DSA = 8, 2, 128
ROT_DSA = HD_DSA // 4
H_IDX, D_IDX = 8, 64
ROT_IDX = D_IDX // 4
TOPK_MAX = 256
CONV_W, CONV_K = 1024, 3
THETA = 500000.0
N_EXPERTS, TOP_K = 8, 2
ALPHA = (2 * DEPTH) ** 0.25
LN_EPS = 1e-5
RMS_EPS = 1e-6
SPLITS = (Q_RANK, KV_RANK, ROPE_MLA, H_DSA * HD_DSA, G_DSA * HD_DSA, G_DSA * HD_DSA,
          H_IDX * D_IDX, D_IDX, H_IDX, CONV_W, CONV_W, CONV_W, 3 * D)

LANES = 128
NEG = -1e30
LOG2E = 1.4426950408889634
VMEM_LIMIT = 56 * 1024 * 1024

MOE_BLK = 512
MOE_GROUP = 2
MLA_BLK = 512


def _params(sem, vmem=VMEM_LIMIT):
    return pltpu.CompilerParams(dimension_semantics=sem, vmem_limit_bytes=vmem)


def _sigmoid(x):
    return 1.0 / (1.0 + jnp.exp(-x))


def _dot(a, b):
    return jnp.dot(a, b, preferred_element_type=F32)


def _dot_t(a, b):
    return lax.dot_general(a, b, (((1,), (1,)), ((), ())), preferred_element_type=F32)


def _layer_norm(y, g, b):
    mu = jnp.mean(y, -1, keepdims=True)
    d = y - mu
    var = jnp.mean(d * d, -1, keepdims=True)
    return d * lax.rsqrt(var + LN_EPS) * g + b


def _rope(y, c, s, half):
    lane = lax.broadcasted_iota(jnp.int32, y.shape, 1)
    swapped = jnp.where(lane < half, pltpu.roll(y, LANES - half, 1), pltpu.roll(y, half, 1))
    return y * c + swapped * s


def _ada_body(c_ref, w_ref, b_ref, o_ref):
    c = c_ref[...]
    ca = (c * _sigmoid(c)).astype(CDT)
    o_ref[...] = _dot(ca, w_ref[...].astype(CDT)) + b_ref[...]


def _ada(c, ada_w, ada_b):
    depth, d, n = ada_w.shape
    bsz = c.shape[0]
    rows = 8
    cp = jnp.zeros((rows, d), F32).at[:bsz].set(c)
    tn = 1024
    out = pl.pallas_call(
        _ada_body,
        grid=(depth, n // tn),
        in_specs=[pl.BlockSpec((rows, d), lambda l, j: (0, 0)),
                  pl.BlockSpec((None, d, tn), lambda l, j: (l, 0, j)),
                  pl.BlockSpec((None, 1, tn), lambda l, j: (l, 0, j))],
        out_specs=pl.BlockSpec((None, rows, tn), lambda l, j: (l, 0, j)),
        out_shape=jax.ShapeDtypeStruct((depth, rows, n), F32),
        compiler_params=_params(("arbitrary", "arbitrary")),
        name="ada_mod",
    )(cp, ada_w, ada_b.reshape(depth, 1, n))
    return out[:, :bsz]


def _mod_body(x_ref, sc_ref, sh_ref, o_ref):
    o_ref[...] = (x_ref[...] * (1.0 + sc_ref[...]) + sh_ref[...]).astype(o_ref.dtype)


def _modulate(x, sc, sh, out_dtype):
    bsz, s_len, d = x.shape
    tm = 512
    vec = pl.BlockSpec((None, 1, d), lambda b, i: (b, 0, 0))
    return pl.pallas_call(
        _mod_body,
        grid=(bsz, s_len // tm),
        in_specs=[pl.BlockSpec((None, tm, d), lambda b, i: (b, i, 0)), vec, vec],
        out_specs=pl.BlockSpec((None, tm, d), lambda b, i: (b, i, 0)),
        out_shape=jax.ShapeDtypeStruct(x.shape, out_dtype),
        compiler_params=_params(("arbitrary", "arbitrary")),
        name="modulate",
    )(x, sc, sh)


def _mla_prep_body(h_ref, w1_ref, qg_ref, kg_ref, wuq_ref, wukv_ref, c_ref, s_ref, q_ref, k_ref, v_ref):
    z = _dot(h_ref[...], w1_ref[...])
    cq = z[:, :Q_RANK]
    ckv = z[:, Q_RANK:Q_RANK + KV_RANK]
    kr = z[:, Q_RANK + KV_RANK:]
    nq = (cq * lax.rsqrt(jnp.mean(cq * cq, -1, keepdims=True) + RMS_EPS) * qg_ref[...]).astype(CDT)
    nkv = (ckv * lax.rsqrt(jnp.mean(ckv * ckv, -1, keepdims=True) + RMS_EPS) * kg_ref[...]).astype(CDT)
    qa = _dot(nq, wuq_ref[...])
    kva = _dot(nkv, wukv_ref[...])
    c = c_ref[...]
    s = s_ref[...]
    scale = (NOPE + ROPE_MLA) ** -0.5 * LOG2E
    krot_t = _rope(kr, c, s, ROPE_MLA // 2).T.astype(CDT)
    lane = lax.broadcasted_iota(jnp.int32, (h_ref.shape[0], LANES), 1)
    ones_col = jnp.where(lane == 0, 1.0, 0.0).astype(CDT)
    for h in range(H_MLA):
        lo = h * 2 * LANES
        q_ref[h, :, 0:LANES] = (qa[:, lo:lo + LANES] * scale).astype(CDT)
        q_ref[h, :, LANES:2 * LANES] = (_rope(qa[:, lo + LANES:lo + 2 * LANES], c, s, ROPE_MLA // 2)
                                        * scale).astype(CDT)
        k_ref[h, 0:LANES, :] = kva[:, lo:lo + LANES].T.astype(CDT)
        k_ref[h, LANES:2 * LANES, :] = krot_t
        v_ref[h, :, 0:LANES] = kva[:, lo + LANES:lo + 2 * LANES].astype(CDT)
        v_ref[h, :, LANES:2 * LANES] = ones_col


def _mla_prep(h, w1, qg, kg, wuq, wukv, ctab, stab):
    bsz, s_len, d = h.shape
    tm = min(MLA_BLK, s_len)
    full = lambda a: pl.BlockSpec(a.shape, lambda b, i: (0,) * a.ndim)
    row = lambda w: pl.BlockSpec((None, tm, w), lambda b, i: (b, i, 0))
    head = lambda w: pl.BlockSpec((None, H_MLA, tm, w), lambda b, i: (b, 0, i, 0))
    return pl.pallas_call(
        _mla_prep_body,
        grid=(bsz, s_len // tm),
        in_specs=[row(d), full(w1), full(qg), full(kg), full(wuq), full(wukv), row(LANES), row(LANES)],
        out_specs=[head(2 * LANES),
                   pl.BlockSpec((None, H_MLA, None, 2 * LANES, tm), lambda b, i: (b, 0, i, 0, 0)),
                   head(2 * LANES)],
        out_shape=[jax.ShapeDtypeStruct((bsz, H_MLA, s_len, 2 * LANES), CDT),
                   jax.ShapeDtypeStruct((bsz, H_MLA, s_len // tm, 2 * LANES, tm), CDT),
                   jax.ShapeDtypeStruct((bsz, H_MLA, s_len, 2 * LANES), CDT)],
        compiler_params=_params(("arbitrary", "arbitrary")),
        name="mla_prep",
    )(h, w1, qg, kg, wuq, wukv, ctab, stab)


def _flash_body(q_ref, k_ref, v_ref, o_ref, s_ref, mx_ref, m_ref, acc_ref, *, blk):
    qi = pl.program_id(2)
    dv = V_MLA

    def scores(hd, j, masked):
        s = _dot(q_ref[hd], k_ref[hd, j])
        if masked:
            row = lax.broadcasted_iota(jnp.int32, s.shape, 0)
            col = lax.broadcasted_iota(jnp.int32, s.shape, 1)
            s = jnp.where(col <= row, s, NEG)
        return s, jnp.max(s, -1, keepdims=True)

    def put(hd, sv):
        s_ref[hd], mx_ref[hd] = sv

    def accumulate(hd, j):
        off = pl.multiple_of(j * blk, blk)
        m_old = m_ref[hd]
        m_new = jnp.maximum(m_old, mx_ref[hd])
        alpha = jnp.exp2(m_old - m_new)
        p = jnp.exp2((s_ref[hd] - m_new).astype(CDT))
        acc_ref[hd] = alpha * acc_ref[hd] + _dot(p, v_ref[hd, pl.ds(off, blk), :])
        m_ref[hd] = m_new

    m_ref[...] = jnp.full(m_ref.shape, NEG, F32)
    acc_ref[...] = jnp.zeros(acc_ref.shape, F32)
    put(0, scores(0, qi, True))
    put(1, scores(1, qi, True))

    def body(n, _):
        j_prev = jnp.where(n == 1, qi, n - 2)
        accumulate(0, j_prev)
        accumulate(1, j_prev)
        put(0, scores(0, n - 1, False))
        put(1, scores(1, n - 1, False))
        return 0

    lax.fori_loop(1, qi + 1, body, 0)
    j_last = jnp.where(qi == 0, qi, qi - 1)
    accumulate(0, j_last)
    accumulate(1, j_last)
    for hd in range(2):
        acc = acc_ref[hd]
        o_ref[:, hd * dv:(hd + 1) * dv] = (acc[:, :dv] / acc[:, dv:dv + 1]).astype(o_ref.dtype)


def _mla_attention(q, k, v):
    bsz, nh, s_len, dq = q.shape
    dve = v.shape[-1]
    dv = V_MLA
    blk = min(MLA_BLK, s_len)
    return pl.pallas_call(
        functools.partial(_flash_body, blk=blk),
        grid=(bsz, nh // 2, s_len // blk),
        in_specs=[pl.BlockSpec((None, 2, blk, dq), lambda b, h, i: (b, h, i, 0)),
                  pl.BlockSpec((None, 2, s_len // blk, dq, blk), lambda b, h, i: (b, h, 0, 0, 0)),
                  pl.BlockSpec((None, 2, s_len, dve), lambda b, h, i: (b, h, 0, 0))],
        out_specs=pl.BlockSpec((None, blk, 2 * dv), lambda b, h, i: (b, i, h)),
        out_shape=jax.ShapeDtypeStruct((bsz, s_len, nh * dv), CDT),
        scratch_shapes=[pltpu.VMEM((2, blk, blk), F32), pltpu.VMEM((2, blk, 1), F32),
                        pltpu.VMEM((2, blk, 1), F32), pltpu.VMEM((2, blk, dve), F32)],
        compiler_params=_params(("arbitrary", "arbitrary", "arbitrary")),
        name="mla_attention",
    )(q, k, v)


DSA_CHUNK = 512
DSA_TQ = 128
VT_PAD = 16
VT_ROWS = HD_DSA + VT_PAD


def _dsa_prep_body(h_ref, w2_ref, cd_ref, sd_ref, ci_ref, si_ref, q_ref, k_ref, vt_ref, qi_ref, ki_ref, wit_ref):
    z = _dot(h_ref[...], w2_ref[...])
    cd, sd, ci, si = cd_ref[...], sd_ref[...], ci_ref[...], si_ref[...]
    col = lambda j: z[:, j * LANES:(j + 1) * LANES]
    for h in range(H_DSA):
        q_ref[h] = (_rope(col(h), cd, sd, ROT_DSA // 2) * (HD_DSA ** -0.5 * LOG2E)).T.astype(CDT)
    for g in range(G_DSA):
        k_ref[:, g * LANES:(g + 1) * LANES] = _rope(col(H_DSA + g), cd, sd, ROT_DSA // 2).astype(CDT)
    base = H_DSA + G_DSA
    sub = lax.broadcasted_iota(jnp.int32, (VT_PAD, z.shape[0]), 0)
    ones_row = jnp.where(sub == 0, 1.0, 0.0).astype(CDT)
    for g in range(G_DSA):
        vt_ref[g * VT_ROWS:g * VT_ROWS + LANES, :] = col(base + g).T.astype(CDT)
        vt_ref[g * VT_ROWS + LANES:(g + 1) * VT_ROWS, :] = ones_row
    base += G_DSA
    for h in range(H_IDX):
        qi_ref[h] = (_rope(col(base + h), ci, si, ROT_IDX // 2) * D_IDX ** -0.5).T.astype(CDT)
    base += H_IDX
    ki_ref[...] = _rope(col(base), ci, si, ROT_IDX // 2).astype(CDT)
    wit_ref[...] = col(base + 1).T[0:H_IDX, :] * H_IDX ** -0.5


def _dsa_prep(h, w2, cd, sd, ci, si):
    bsz, s_len, d = h.shape
    tm = min(DSA_CHUNK, s_len)
    row = lambda w: pl.BlockSpec((None, tm, w), lambda b, i: (b, i, 0))
    return pl.pallas_call(
        _dsa_prep_body,
        grid=(bsz, s_len // tm),
        in_specs=[row(d), pl.BlockSpec(w2.shape, lambda b, i: (0, 0)), row(LANES), row(LANES), row(LANES),
                  row(LANES)],
        out_specs=[pl.BlockSpec((None, H_DSA, LANES, tm), lambda b, i: (b, 0, 0, i)), row(G_DSA * LANES),
                   pl.BlockSpec((None, None, G_DSA * VT_ROWS, tm), lambda b, i: (b, i, 0, 0)),
                   pl.BlockSpec((None, H_IDX, LANES, tm), lambda b, i: (b, 0, 0, i)), row(LANES),
                   pl.BlockSpec((None, H_IDX, tm), lambda b, i: (b, 0, i))],
        out_shape=[jax.ShapeDtypeStruct((bsz, H_DSA, LANES, s_len), CDT),
                   jax.ShapeDtypeStruct((bsz, s_len, G_DSA * LANES), CDT),
                   jax.ShapeDtypeStruct((bsz, s_len // tm, G_DSA * VT_ROWS, tm), CDT),
                   jax.ShapeDtypeStruct((bsz, H_IDX, LANES, s_len), CDT),
                   jax.ShapeDtypeStruct((bsz, s_len, LANES), CDT),
                   jax.ShapeDtypeStruct((bsz, H_IDX, s_len), F32)],
        compiler_params=_params(("arbitrary", "arbitrary")),
        name="dsa_prep",
    )(h, w2, cd, sd, ci, si)


PDT = jnp.bfloat16
PACK_ROWS = 16


def _dsa_body(qi_ref, ki_ref, wit_ref, q_ref, k_ref, vt_ref, tri_ref, o_ref,
              key_ref, dig_ref, tie_ref, sel_ref, qia_ref, qg_ref, s_ref, mx_ref, m_ref, acc_ref, *, tq, tkc, n_sel):
    i = pl.program_id(1)
    q0 = i * tq
    nkc = (q0 + tq + tkc - 1) // tkc
    n_rep = H_DSA // G_DSA
    key_neg_inf = jnp.int32(-8388608) ^ jnp.int32(0x7FFFFFFF)
    qpos = q0 + lax.broadcasted_iota(jnp.int32, (tkc, tq), 1)

    for h in range(H_IDX):
        qia_ref[:, h * tq:(h + 1) * tq] = qi_ref[h]
    for g in range(G_DSA):
        for r in range(n_rep):
            qg_ref[g, :, r * tq:(r + 1) * tq] = q_ref[g * n_rep + r]

    def score_chunk(c, _):
        off = pl.multiple_of(c * tkc, tkc)
        kc = ki_ref[pl.ds(off, tkc), :]
        w = wit_ref[...]
        lg = _dot(kc, qia_ref[...])
        sc = jnp.zeros((tkc, tq), F32)
        for h in range(H_IDX):
            sc = sc + w[h:h + 1, :] * jnp.maximum(lg[:, h * tq:(h + 1) * tq], 0.0)
        kpos = off + lax.broadcasted_iota(jnp.int32, (tkc, tq), 0)
        sc = jnp.where(kpos <= qpos, sc, -jnp.inf)
        bits = pltpu.bitcast(sc, jnp.int32)
        key = jnp.where(bits >= 0, bits, bits ^ jnp.int32(0x7FFFFFFF))
        key_ref[c] = key
        dig_ref[0, c] = small(lax.shift_right_arithmetic(key, jnp.int32(24)) + 128)
        for p in range(1, 4):
            dig_ref[p, c] = small(lax.shift_right_arithmetic(key, jnp.int32(24 - 8 * p)) & jnp.int32(0xFF))
        return 0

    small = lambda v: v.astype(F32).astype(PDT)
    lax.fori_loop(0, nkc, score_chunk, 0)

    one, zero = jnp.ones((), PDT), jnp.zeros((), PDT)
    sub = PACK_ROWS

    def count(get, pred):
        def body(c, cnt):
            ind = jnp.where(pred(get(c)), one, zero)
            parts = [ind[j * sub:(j + 1) * sub, :] for j in range(tkc // sub)]
            while len(parts) > 1:
                parts = [parts[j] + parts[j + 1] for j in range(0, len(parts), 2)]
            return cnt + parts[0]
        part = lax.fori_loop(0, nkc, body, jnp.zeros((sub, tq), PDT))
        return jnp.sum(part.astype(F32), axis=0, keepdims=True)

    def byte_search(get, want):
        def step(bi, t):
            cand = t + lax.shift_left(jnp.int32(1), 7 - bi)
            return jnp.where(count(get, lambda v: v >= small(cand)) >= want, cand, t)
        t = lax.fori_loop(0, 8, step, jnp.zeros((1, tq), jnp.int32))
        return t, count(get, lambda v: v > small(t))

    top = lambda c: dig_ref[0, c]
    tied = lambda c: tie_ref[c]
    t, cnt_gt = byte_search(top, n_sel)
    thr = t - 128
    for p in range(1, 4):
        prev = top if p == 1 else tied

        def narrow(c, _, prev=prev, p=p, t=t):
            tie_ref[c] = jnp.where(prev(c) == small(t), dig_ref[p, c], -one)
            return 0
        lax.fori_loop(0, nkc, narrow, 0)
        t, gt = byte_search(tied, n_sel - cnt_gt)
        thr = thr * 256 + t
        cnt_gt = cnt_gt + gt

    cnt_eq = count(tied, lambda v: v == small(t))
    need = n_sel - cnt_gt
    excess = jnp.where(thr > key_neg_inf, cnt_eq - need, 0.0)

    def mask_plain():
        def body(c, _):
            kk = key_ref[c]
            sel_ref[c] = jnp.where(kk >= thr, jnp.where(kk > key_neg_inf, 0.0, NEG), NEG)
            return 0
        lax.fori_loop(0, nkc, body, 0)

    def mask_ranked():
        def body(c, seen):
            kk = key_ref[c]
            tie = kk == thr
            pre = _dot(tri_ref[...], jnp.where(tie, 1.0, 0.0).astype(CDT))
            rank = seen + pre
            keep_tie = jnp.where(tie, jnp.where(rank <= need, 0.0, NEG), NEG)
            keep = jnp.where(kk > thr, 0.0, keep_tie)
            sel_ref[c] = jnp.where(kk > key_neg_inf, keep, NEG)
            return seen + pre[tkc - 1:tkc, :]
        lax.fori_loop(0, nkc, body, jnp.zeros((1, tq), F32))

    lax.cond(jnp.max(excess) > 0.0, mask_ranked, mask_plain)

    m_ref[...] = jnp.full(m_ref.shape, NEG, F32)
    acc_ref[...] = jnp.zeros(acc_ref.shape, F32)

    def scores(g, c):
        off = pl.multiple_of(c * tkc, tkc)
        msk = sel_ref[c]
        msk = jnp.concatenate([msk] * n_rep, axis=1)
        kg = k_ref[pl.ds(off, tkc), g * LANES:(g + 1) * LANES]
        s = _dot(kg, qg_ref[g]) + msk
        return s, jnp.max(s, axis=0, keepdims=True)

    def put(g, sv):
        s_ref[g], mx_ref[g] = sv

    def accumulate(g, c):
        m_old = m_ref[g]
        m_new = jnp.maximum(m_old, mx_ref[g])
        alpha = jnp.exp2(m_old - m_new)
        p = jnp.exp2((s_ref[g] - m_new).astype(CDT))
        acc_ref[g] = alpha * acc_ref[g] + _dot(vt_ref[c, g * VT_ROWS:(g + 1) * VT_ROWS, :], p)
        m_ref[g] = m_new

    put(0, scores(0, 0))
    put(1, scores(1, 0))

    def att_chunk(c, _):
        sv0 = scores(0, c)
        sv1 = scores(1, c)
        accumulate(0, c - 1)
        accumulate(1, c - 1)
        put(0, sv0)
        put(1, sv1)
        return 0

    lax.fori_loop(1, nkc, att_chunk, 0)
    accumulate(0, nkc - 1)
    accumulate(1, nkc - 1)
    for g in range(G_DSA):
        acc = acc_ref[g]
        out = acc[0:HD_DSA, :] / acc[HD_DSA:HD_DSA + 1, :]
        for r in range(n_rep):
            hh = g * n_rep + r
            o_ref[:, hh * LANES:(hh + 1) * LANES] = out[:, r * tq:(r + 1) * tq].T.astype(o_ref.dtype)


def _dsa_attention(qi, ki, wit, q, k, vt):
    bsz, s_len, _ = k.shape
    tq = min(DSA_TQ, s_len)
    tkc = min(DSA_CHUNK, s_len)
    n_chunks = s_len // tkc
    n_sel = min(TOPK_MAX, s_len // 4)
    n_rep = H_DSA // G_DSA
    assert s_len // PACK_ROWS <= 256, "packed count partial sums must stay exactly representable in bf16"
    tri =(lax.broadcasted_iota(jnp.int32, (tkc, tkc), 1)
           <= lax.broadcasted_iota(jnp.int32, (tkc, tkc), 0)).astype(CDT)
    row = lambda w: pl.BlockSpec((None, tq, w), lambda b, i: (b, i, 0))
    seq = lambda w: pl.BlockSpec((None, s_len, w), lambda b, i: (b, 0, 0))
    return pl.pallas_call(
        functools.partial(_dsa_body, tq=tq, tkc=tkc, n_sel=n_sel),
        grid=(bsz, s_len // tq),
        in_specs=[pl.BlockSpec((None, H_IDX, LANES, tq), lambda b, i: (b, 0, 0, i)), seq(LANES),
                  pl.BlockSpec((None, H_IDX, tq), lambda b, i: (b, 0, i)),
                  pl.BlockSpec((None, H_DSA, LANES, tq), lambda b, i: (b, 0, 0, i)), seq(G_DSA * LANES),
                  pl.BlockSpec((None, n_chunks, G_DSA * VT_ROWS, tkc), lambda b, i: (b, 0, 0, 0)),
                  pl.BlockSpec((tkc, tkc), lambda b, i: (0, 0))],
        out_specs=row(H_DSA * LANES),
        out_shape=jax.ShapeDtypeStruct((bsz, s_len, H_DSA * LANES), CDT),
        scratch_shapes=[pltpu.VMEM((n_chunks, tkc, tq), jnp.int32),
                        pltpu.VMEM((4, n_chunks, tkc, tq), PDT),
                        pltpu.VMEM((n_chunks, tkc, tq), PDT),
                        pltpu.VMEM((n_chunks, tkc, tq), F32),
                        pltpu.VMEM((LANES, H_IDX * tq), CDT),
                        pltpu.VMEM((G_DSA, LANES, n_rep * tq), CDT),
                        pltpu.VMEM((G_DSA, tkc, n_rep * tq), F32),
                        pltpu.VMEM((G_DSA, 1, n_rep * tq), F32),
                        pltpu.VMEM((G_DSA, 1, n_rep * tq), F32),
                        pltpu.VMEM((G_DSA, VT_ROWS, n_rep * tq), F32)],
        compiler_params=_params(("arbitrary", "arbitrary")),
        name="dsa_attention",
    )(qi, ki, wit, q, k, vt, tri)


def _conv_body(h_ref, w_ref, cw_ref, cb_ref, o_ref, tail_ref, *, tm, tc):
    @pl.when(pl.program_id(2) == 0)
    def _():
        tail_ref[...] = jnp.zeros(tail_ref.shape, F32)

    z = _dot(h_ref[...], w_ref[...])
    gb = z[:, :tc]
    u = z[:, tc:2 * tc] * z[:, 2 * tc:]
    prev = tail_ref[...]
    row = lax.broadcasted_iota(jnp.int32, (tm, tc), 0)
    u1 = jnp.where(row == 0, prev[7:8, :], pltpu.roll(u, 1, 0))
    u2 = jnp.where(row == 0, prev[6:7, :], jnp.where(row == 1, prev[7:8, :], pltpu.roll(u, 2, 0)))
    cw = cw_ref[...]
    y = cw[0:1, :] * u2 + cw[1:2, :] * u1 + cw[2:3, :] * u + cb_ref[...]
    o_ref[...] = (gb * y).astype(o_ref.dtype)
    tail_ref[...] = u[tm - 8:tm, :]


def _conv_branch(h, w3, conv_w, conv_b, tc):
    bsz, s_len, d = h.shape
    tm = 512
    n_cb = CONV_W // tc
    return pl.pallas_call(
        functools.partial(_conv_body, tm=tm, tc=tc),
        grid=(n_cb, bsz, s_len // tm),
        in_specs=[pl.BlockSpec((None, tm, d), lambda cb, b, i: (b, i, 0)),
                  pl.BlockSpec((d, 3 * tc), lambda cb, b, i: (0, cb)),
                  pl.BlockSpec((CONV_K, tc), lambda cb, b, i: (0, cb)),
                  pl.BlockSpec((1, tc), lambda cb, b, i: (0, cb))],
        out_specs=pl.BlockSpec((None, tm, tc), lambda cb, b, i: (b, i, cb)),
        out_shape=jax.ShapeDtypeStruct((bsz, s_len, CONV_W), CDT),
        scratch_shapes=[pltpu.VMEM((8, tc), F32)],
        compiler_params=_params(("arbitrary", "arbitrary", "arbitrary")),
        name="conv_branch",
    )(h, w3, conv_w, conv_b.reshape(1, CONV_W))


def _merge_body(h_ref, ya_ref, yb_ref, yc_ref, wg_ref, wbr_ref, o_ref):
    h = h_ref[...]
    acc = None
    for j, y_ref in enumerate((ya_ref, yb_ref, yc_ref)):
        term = _sigmoid(_dot(h, wg_ref[j])) * _dot(y_ref[...], wbr_ref[j])
        acc = term if acc is None else acc + term
    o_ref[...] = acc.astype(o_ref.dtype)


def _merge(h, ya, yb, yc, wg, wbr):
    t, d = h.shape
    tm, tn = 512, 512
    yw = ya.shape[1]
    row = lambda w: pl.BlockSpec((tm, w), lambda j, i: (i, 0))
    return pl.pallas_call(
        _merge_body,
        grid=(d // tn, t // tm),
        in_specs=[row(d), row(yw), row(yw), row(yw),
                  pl.BlockSpec((3, d, tn), lambda j, i: (0, 0, j)),
                  pl.BlockSpec((3, yw, tn), lambda j, i: (0, 0, j))],
        out_specs=pl.BlockSpec((tm, tn), lambda j, i: (i, j)),
        out_shape=jax.ShapeDtypeStruct((t, d), CDT),
        compiler_params=_params(("arbitrary", "arbitrary")),
        name="gated_merge",
    )(h, ya, yb, yc, wg, wbr)


def _out_ln_body(m_ref, wo_ref, x_ref, gm_ref, lg_ref, lb_ref, sc_ref, sh_ref, x1_ref, h_ref):
    y = ALPHA * x_ref[...] + gm_ref[...] * _dot(m_ref[...], wo_ref[...])
    x1 = _layer_norm(y, lg_ref[...], lb_ref[...])
    x1_ref[...] = x1
    h_ref[...] = (x1 * (1.0 + sc_ref[...]) + sh_ref[...]).astype(h_ref.dtype)


def _out_ln(merged, wo, x, gm, lg, lb, sc, sh, h_dtype):
    bsz, s_len, d = x.shape
    tm = 256
    row = pl.BlockSpec((None, tm, d), lambda b, i: (b, i, 0))
    per_b = pl.BlockSpec((None, 1, d), lambda b, i: (b, 0, 0))
    vec = pl.BlockSpec((1, d), lambda b, i: (0, 0))
    return pl.pallas_call(
        _out_ln_body,
        grid=(bsz, s_len // tm),
        in_specs=[row, pl.BlockSpec((d, d), lambda b, i: (0, 0)), row, per_b, vec, vec, per_b, per_b],
        out_specs=[row, row],
        out_shape=[jax.ShapeDtypeStruct(x.shape, F32), jax.ShapeDtypeStruct(x.shape, h_dtype)],
        compiler_params=_params(("arbitrary", "arbitrary")),
        name="out_proj_ln",
    )(merged, wo, x, gm, lg, lb, sc, sh)


def _ffn_body(h_ref, wg_ref, wu_ref, wd_ref, x_ref, gf_ref, lg_ref, lb_ref, sc_ref, sh_ref, x2_ref, h2_ref,
              acc_ref):
    k = pl.program_id(2)

    @pl.when(k == 0)
    def _():
        acc_ref[...] = jnp.zeros(acc_ref.shape, F32)

    h = h_ref[...]
    a = _dot(h, wg_ref[...])
    act = (a * _sigmoid(a) * _dot(h, wu_ref[...])).astype(CDT)
    acc_ref[...] += _dot(act, wd_ref[...])

    @pl.when(k == pl.num_programs(2) - 1)
    def _():
        y = ALPHA * x_ref[...] + gf_ref[...] * acc_ref[...]
        x2 = _layer_norm(y, lg_ref[...], lb_ref[...])
        x2_ref[...] = x2
        h2_ref[...] = (x2 * (1.0 + sc_ref[...]) + sh_ref[...]).astype(h2_ref.dtype)


def _ffn_ln(h, wg, wu, wd, x, gf, lg, lb, sc, sh):
    bsz, s_len, d = x.shape
    f = wg.shape[1]
    tm, tf = 512, 512
    row = pl.BlockSpec((None, tm, d), lambda b, i, k: (b, i, 0))
    per_b = pl.BlockSpec((None, 1, d), lambda b, i, k: (b, 0, 0))
    vec = pl.BlockSpec((1, d), lambda b, i, k: (0, 0))
    return pl.pallas_call(
        _ffn_body,
        grid=(bsz, s_len // tm, f // tf),
        in_specs=[row, pl.BlockSpec((d, tf), lambda b, i, k: (0, k)), pl.BlockSpec((d, tf), lambda b, i, k: (0, k)),
                  pl.BlockSpec((tf, d), lambda b, i, k: (k, 0)), row, per_b, vec, vec, per_b, per_b],
        out_specs=[row, row],
        out_shape=[jax.ShapeDtypeStruct(x.shape, F32), jax.ShapeDtypeStruct(x.shape, CDT)],
        scratch_shapes=[pltpu.VMEM((tm, d), F32)],
        compiler_params=_params(("arbitrary", "arbitrary", "arbitrary")),
        name="ffn_ln",
    )(h, wg, wu, wd, x, gf, lg, lb, sc, sh)


def _router_body(h_ref, wr_ref, e_ref, g_ref):
    lg = jnp.dot(h_ref[...], wr_ref[...], preferred_element_type=F32, precision=lax.Precision.HIGHEST)
    lane = lax.broadcasted_iota(jnp.int32, lg.shape, 1)
    lanef = lane.astype(F32)
    lg = jnp.where(lane < N_EXPERTS, lg, -jnp.inf)
    m1 = jnp.max(lg, -1, keepdims=True)
    i1 = jnp.min(jnp.where(lg == m1, lanef, float(LANES)), -1, keepdims=True)
    lg2 = jnp.where(lanef == i1, -jnp.inf, lg)
    m2 = jnp.max(lg2, -1, keepdims=True)
    i2 = jnp.min(jnp.where(lg2 == m2, lanef, float(LANES)), -1, keepdims=True)
    e = jnp.exp(m2 - m1)
    g1 = 1.0 / (1.0 + e)
    g2 = e / (1.0 + e)
    e_ref[...] = jnp.where(lane == 0, i1, jnp.where(lane == 1, i2, 0.0)).astype(jnp.int32)
    g_ref[...] = jnp.where(lane == 0, g1, jnp.where(lane == 1, g2, 0.0))


def _router(h, wr):
    t, d = h.shape
    tm = 512
    return pl.pallas_call(
        _router_body,
        grid=(t // tm,),
        in_specs=[pl.BlockSpec((tm, d), lambda i: (i, 0)), pl.BlockSpec((d, LANES), lambda i: (0, 0))],
        out_specs=[pl.BlockSpec((tm, LANES), lambda i: (i, 0)), pl.BlockSpec((tm, LANES), lambda i: (i, 0))],
        out_shape=[jax.ShapeDtypeStruct((t, LANES), jnp.int32), jax.ShapeDtypeStruct((t, LANES), F32)],
        compiler_params=_params(("arbitrary",)),
        name="moe_router",
    )(h, wr)


def _moe_body(tok_ref, ge_ref, bv_ref, h_hbm, wg_ref, wu_ref, wd_ref, y_hbm, xbuf, xs, y_ref, sem, out_sem, *,
              blk, per, n_steps):
    i = pl.program_id(0)
    k = pl.program_id(1)
    ng = pl.num_programs(0)
    nk = pl.num_programs(1)
    rows = MOE_GROUP * blk

    def out_copy(gi):
        return pltpu.make_async_copy(y_ref, y_hbm.at[pl.ds(pl.multiple_of(gi * rows, rows), rows), :], out_sem)

    def row_copy(gi, r):
        return pltpu.make_async_copy(h_hbm.at[pl.ds(tok_ref[gi * rows + r], 1), :], xbuf.at[pl.ds(r, 1), :], sem)

    def wait_rows():
        pltpu.make_async_copy(h_hbm.at[pl.ds(0, per * n_steps), :], xbuf, sem).wait()

    @pl.when(k == 0)
    def _():
        @pl.when(i == 0)
        def _():
            def body(r, _):
                row_copy(0, r).start()
                return 0
            lax.fori_loop(0, per * n_steps, body, 0)
            wait_rows()

        @pl.when((i > 0) & (bv_ref[jnp.maximum(i - 1, 0) * MOE_GROUP] == 1))
        def _():
            wait_rows()

        xs[...] = xbuf[0:rows, :].astype(CDT)

        @pl.when(i > 0)
        def _():
            out_copy(i - 1).wait()

        y_ref[...] = jnp.zeros(y_ref.shape, F32)

    def ffn(n_blocks):
        for j in range(per):
            row_copy(i + 1, k * per + j).start()
        wg = wg_ref[...].astype(CDT)
        wu = wu_ref[...].astype(CDT)
        wd = wd_ref[...].astype(CDT)
        for b in range(n_blocks):
            x = xs[b * blk:(b + 1) * blk, :]
            a = _dot(x, wg)
            act = (a * _sigmoid(a) * _dot(x, wu)).astype(CDT)
            y_ref[b * blk:(b + 1) * blk, :] += _dot(act, wd)

    live = bv_ref[i * MOE_GROUP]
    for b in range(1, MOE_GROUP):
        live = live + bv_ref[i * MOE_GROUP + b]
    for n_blocks in range(1, MOE_GROUP + 1):
        pl.when(live == n_blocks)(functools.partial(ffn, n_blocks))

    @pl.when(k == nk - 1)
    def _():
        out_copy(i).start()

        @pl.when(i == ng - 1)
        def _():
            out_copy(i).wait()


def _moe_ffn(h, row_tok, grp_e, blk_valid, wg, wu, wd):
    t, d = h.shape
    n_rows = row_tok.shape[0]
    blk = MOE_BLK
    rows = MOE_GROUP * blk
    ng = n_rows // rows
    f = wg.shape[2]
    tf = 512
    nk = f // tf
    sublanes = 8
    per = -(-(-(-rows // nk)) // sublanes) * sublanes
    row_tok = jnp.pad(row_tok, (0, per * nk - rows))
    kk = lambda k, i, bv: jnp.where(bv[i * MOE_GROUP] == 1, k, nk - 1)
    grid_spec = pltpu.PrefetchScalarGridSpec(
        num_scalar_prefetch=3,
        grid=(ng, nk),
        in_specs=[pl.BlockSpec(memory_space=pl.ANY),
                  pl.BlockSpec((None, d, tf), lambda i, k, tok, ge, bv: (ge[i], 0, kk(k, i, bv))),
                  pl.BlockSpec((None, d, tf), lambda i, k, tok, ge, bv: (ge[i], 0, kk(k, i, bv))),
                  pl.BlockSpec((None, tf, d), lambda i, k, tok, ge, bv: (ge[i], kk(k, i, bv), 0))],
        out_specs=pl.BlockSpec(memory_space=pl.ANY),
        scratch_shapes=[pltpu.VMEM((per * nk, d), F32), pltpu.VMEM((rows, d), CDT), pltpu.VMEM((rows, d), F32),
                        pltpu.SemaphoreType.DMA(()), pltpu.SemaphoreType.DMA(())],
    )
    return pl.pallas_call(
        functools.partial(_moe_body, blk=blk, per=per, n_steps=nk),
        grid_spec=grid_spec,
        out_shape=jax.ShapeDtypeStruct((n_rows, d), F32),
        compiler_params=_params(("arbitrary", "arbitrary")),
        name="moe_grouped_ffn",
    )(row_tok, grp_e, blk_valid, h, wg, wu, wd)


def _combine_body(pos_ref, y_hbm, gate_ref, x_ref, gf_ref, lg_ref, lb_ref, o_ref, ybuf, sem, *, tm):
    i = pl.program_id(0)
    nb = pl.num_programs(0)

    def issue(bi, slot):
        def body(r, _):
            for s in range(TOP_K):
                p = pos_ref[(bi * tm + r) * TOP_K + s]
                pltpu.make_async_copy(y_hbm.at[pl.ds(p, 1), :], ybuf.at[slot, s, pl.ds(r, 1), :],
                                      sem.at[slot]).start()
            return 0
        lax.fori_loop(0, tm, body, 0)

    @pl.when(i == 0)
    def _():
        issue(0, 0)

    slot = i % 2
    for s in range(TOP_K):
        pltpu.make_async_copy(y_hbm.at[pl.ds(0, tm), :], ybuf.at[slot, s], sem.at[slot]).wait()

    @pl.when(i + 1 < nb)
    def _():
        issue(i + 1, (i + 1) % 2)

    gate = gate_ref[...]
    f = ybuf[slot, 0] * gate[:, 0:1] + ybuf[slot, 1] * gate[:, 1:2]
    y = ALPHA * x_ref[...] + gf_ref[...] * f
    o_ref[...] = _layer_norm(y, lg_ref[...], lb_ref[...])


def _combine_ln(y_rows, pos, gate, x, gf, lg, lb):
    bsz, s_len, d = x.shape
    tm = 256
    nsb = s_len // tm
    grid_spec = pltpu.PrefetchScalarGridSpec(
        num_scalar_prefetch=1,
        grid=(bsz * nsb,),
        in_specs=[pl.BlockSpec(memory_space=pl.ANY),
                  pl.BlockSpec((tm, LANES), lambda i, pos: (i, 0)),
                  pl.BlockSpec((None, tm, d), lambda i, pos: (i // nsb, i % nsb, 0)),
                  pl.BlockSpec((None, 1, d), lambda i, pos: (i // nsb, 0, 0)),
                  pl.BlockSpec((1, d), lambda i, pos: (0, 0)),
                  pl.BlockSpec((1, d), lambda i, pos: (0, 0))],
        out_specs=pl.BlockSpec((None, tm, d), lambda i, pos: (i // nsb, i % nsb, 0)),
        scratch_shapes=[pltpu.VMEM((2, TOP_K, tm, d), F32), pltpu.SemaphoreType.DMA((2,))],
    )
    return pl.pallas_call(
        functools.partial(_combine_body, tm=tm),
        grid_spec=grid_spec,
        out_shape=jax.ShapeDtypeStruct(x.shape, F32),
        compiler_params=_params(("arbitrary",)),
        name="moe_combine_ln",
    )(pos, y_rows, gate, x, gf, lg, lb)


def _routing_tables(top_e, n_tok):
    blk = MOE_BLK
    rows = MOE_GROUP * blk
    flat_e = top_e.reshape(-1)
    onehot = (flat_e[:, None] == jnp.arange(N_EXPERTS, dtype=jnp.int32)[None, :]).astype(jnp.int32)
    csum = jnp.cumsum(onehot, axis=0)
    counts = csum[-1]
    rank = jnp.sum(csum * onehot, axis=1) - 1
    padded = (counts + rows - 1) // rows * rows
    pend = jnp.cumsum(padded)
    pstart = pend - padded
    pos = (jnp.sum(pstart[None, :] * onehot, axis=1) + rank).astype(jnp.int32)
    n_rows = ((n_tok * TOP_K + N_EXPERTS * (rows - 1)) // rows + 1) * rows
    flat_tok = jnp.repeat(jnp.arange(n_tok, dtype=jnp.int32), TOP_K)
    row_tok = jnp.zeros((n_rows,), jnp.int32).at[pos].set(flat_tok)
    grp_start = jnp.arange(n_rows // rows, dtype=jnp.int32) * rows
    last_start = jnp.maximum(pend[-1] - rows, 0)
    grp_e = jnp.sum((pend[None, :] <= jnp.minimum(grp_start, last_start)[:, None]).astype(jnp.int32), axis=1)
    grp_e = jnp.minimum(grp_e, N_EXPERTS - 1)
    blk_start = jnp.arange(n_rows // blk, dtype=jnp.int32) * blk
    blk_e = jnp.repeat(grp_e, MOE_GROUP)
    blk_valid = ((blk_start < pend[-1]) & (blk_start - pstart[blk_e] < counts[blk_e])).astype(jnp.int32)
    return pos, row_tok, grp_e, blk_valid


def _rope_tables(positions, rot_dim, pad_value):
    half = rot_dim // 2
    inv = jnp.asarray(THETA ** (-np.arange(0, rot_dim, 2, dtype=np.float32) / np.float32(rot_dim)), F32)
    ang = positions.astype(F32)[..., None] * inv
    cos, sin = jnp.cos(ang), jnp.sin(ang)
    rest = positions.shape + (LANES - 2 * half,)
    c = jnp.concatenate([cos, cos, jnp.full(rest, pad_value, F32)], -1)
    s = jnp.concatenate([-sin, sin, jnp.zeros(rest, F32)], -1)
    return c, s


def _pad_cols(w, width):
    return jnp.pad(w, ((0, 0), (0, width - w.shape[1])))


def _mixer_weights(w_in, w_uq, w_ukv, w_a, w_b, w_c, tc):
    offs = [0]
    for n in SPLITS:
        offs.append(offs[-1] + n)
    part = lambda j: w_in[:, offs[j]:offs[j + 1]]
    d = w_in.shape[0]
    w1 = _pad_cols(jnp.concatenate([part(0), part(1), part(2)], 1), Q_RANK + KV_RANK + LANES)
    qi = jnp.pad(part(6).reshape(d, H_IDX, D_IDX), ((0, 0), (0, 0), (0, LANES - D_IDX))).reshape(d, H_IDX * LANES)
    w2 = jnp.concatenate([part(3), part(4), part(5), qi, _pad_cols(part(7), LANES), _pad_cols(part(8), LANES)], 1)
    n_cb = CONV_W // tc
    w3 = jnp.stack([part(9).reshape(d, n_cb, tc), part(10).reshape(d, n_cb, tc), part(11).reshape(d, n_cb, tc)],
                   axis=2).reshape(d, 3 * CONV_W)
    wg = part(12).reshape(d, 3, d).transpose(1, 0, 2)
    wuq = jnp.pad(w_uq.reshape(Q_RANK, H_MLA, NOPE + ROPE_MLA),
                  ((0, 0), (0, 0), (0, 2 * LANES - NOPE - ROPE_MLA))).reshape(Q_RANK, H_MLA * 2 * LANES)
    wbr = jnp.stack([w_a, w_b, w_c], 0)
    cast = lambda a: a.astype(CDT)
    return cast(w1), cast(w2), cast(w3), cast(wg), cast(wuq), cast(w_ukv), cast(wbr)


def kernel(x, c, positions, ada_w, ada_b, ln1_g, ln1_b, ln2_g, ln2_b, w_in, mla_q_norm, mla_kv_norm, w_uq, w_ukv,
           conv_w, conv_b, w_branch_a, w_branch_b, w_branch_c, w_o, ffn_w_gate, ffn_w_up, ffn_w_down, router_w,
           moe_w_gate, moe_w_up, moe_w_down):
    bsz, s_len, d = x.shape
    n_tok = bsz * s_len
    depth = ada_w.shape[0]
    conv_tc = 512

    cm, sm = _rope_tables(positions, ROPE_MLA, 0.0)
    cd, sd = _rope_tables(positions, ROT_DSA, 1.0)
    ci, si = _rope_tables(positions, ROT_IDX, 1.0)

    mod = _ada(c, ada_w, ada_b)
    vec = lambda a: a.reshape(1, d)

    h = None
    for i in range(depth):
        sh_m, sc_m, g_m, sh_f, sc_f, g_f = [mod[i, :, j * d:(j + 1) * d].reshape(bsz, 1, d) for j in range(6)]
        if h is None:
            h = _modulate(x, sc_m, sh_m, CDT)
        w1, w2, w3, wg, wuq, wukv, wbr = _mixer_weights(w_in[i], w_uq[i], w_ukv[i], w_branch_a[i], w_branch_b[i],
                                                       w_branch_c[i], conv_tc)

        qa, ka, va = _mla_prep(h, w1, mla_q_norm[i].reshape(1, -1), mla_kv_norm[i].reshape(1, -1), wuq, wukv, cm, sm)
        ya = _mla_attention(qa, ka, va)

        qd, kd, vd, qi, ki, wi = _dsa_prep(h, w2, cd, sd, ci, si)
        yb = _dsa_attention(qi, ki, wi, qd, kd, vd)

        yc = _conv_branch(h, w3, conv_w[i], conv_b[i], conv_tc)

        merged = _merge(h.reshape(n_tok, d), ya.reshape(n_tok, -1), yb.reshape(n_tok, -1), yc.reshape(n_tok, -1),
                        wg, wbr)
        moe_layer = i % 2 == 1
        x, h = _out_ln(merged.reshape(bsz, s_len, d), w_o[i].astype(CDT), x, g_m, vec(ln1_g[i]), vec(ln1_b[i]),
                       sc_f, sh_f, F32 if moe_layer else CDT)

        j = i // 2
        if i + 1 < depth:
            sh_n, sc_n = [mod[i + 1, :, q * d:(q + 1) * d].reshape(bsz, 1, d) for q in range(2)]
        else:
            sh_n, sc_n = jnp.zeros((bsz, 1, d), F32), jnp.zeros((bsz, 1, d), F32)
        if not moe_layer:
            x, h = _ffn_ln(h, ffn_w_gate[j].astype(CDT), ffn_w_up[j].astype(CDT), ffn_w_down[j].astype(CDT), x, g_f,
                           vec(ln2_g[i]), vec(ln2_b[i]), sc_n, sh_n)
        else:
            ht = h.reshape(n_tok, d)
            top_e, top_g = _router(ht, _pad_cols(router_w[j], LANES))
            pos, row_tok, grp_e, blk_valid = _routing_tables(top_e[:, :TOP_K], n_tok)
            y_rows = _moe_ffn(ht, row_tok, grp_e, blk_valid, moe_w_gate[j], moe_w_up[j], moe_w_down[j])
            x = _combine_ln(y_rows, pos, top_g, x, g_f, vec(ln2_g[i]), vec(ln2_b[i]))
            h = None if i + 1 >= depth else _modulate(x, sc_n, sh_n, CDT)
    return x
```
